```python
import math
import jax, jax.numpy as jnp
from jax import lax
import numpy as np

D_MODEL = 4096
BATCH = 4
SEQ = 2048
DEPTH = 2
DEC_BATCH = 8
DEC_SEQ = 4
PAST_LEN = 16384
PAGE_SIZE = 128

HEAD_DIM = 128
D_MIX = D_MODEL
D_CONV = D_MIX // 4
CONV_WIDTH = 3
N_HEADS_DIFF = (3 * D_MIX // 8) // HEAD_DIM
N_KV_DIFF = 4
D_DIFF = N_HEADS_DIFF * HEAD_DIM
N_HEADS_DSA = (3 * D_MIX // 8) // HEAD_DIM
N_KV_DSA = 2
D_DSA = N_HEADS_DSA * HEAD_DIM
N_IDX_HEADS = 16
D_IDX = 64
TOPK_MAX = 256
Q_BLOCK = 128
RMS_EPS = 1e-6
LAMBDA_INIT_BASE = 0.8
LAMBDA_INIT_AMP = 0.6
LAMBDA_INIT_RATE = 0.3
SPLIT_SIZES = (D_CONV, D_CONV, D_CONV,
               D_DIFF, N_KV_DIFF * HEAD_DIM, N_KV_DIFF * HEAD_DIM,
               D_DSA, N_KV_DSA * HEAD_DIM, N_KV_DSA * HEAD_DIM,
               N_IDX_HEADS * D_IDX, D_IDX, N_IDX_HEADS,
               D_MIX)
D_IN = sum(SPLIT_SIZES)

kernel_name = 'hybrid_conv_diffattn_dsa_step'


def _rmsnorm(x, g):
    xf = x.astype(jnp.float32)
    y = xf * lax.rsqrt(jnp.mean(xf * xf, axis=-1, keepdims=True) + RMS_EPS)
    return (y * g.astype(jnp.float32)).astype(x.dtype)


def _pre_mix(x, c, norm_g, w_ada, b_ada, w_in):
    mod = (jax.nn.silu(c) @ w_ada + b_ada)[:, None, :]
    shift, scale, gate = jnp.split(mod, 3, axis=-1)
    h = _rmsnorm(x, norm_g) * (1 + scale) + shift
    cuts = np.cumsum(SPLIT_SIZES)[:-1].tolist()
    return jnp.split(h @ w_in, cuts, axis=-1), gate


def _causal_conv(u_ext, w):
    t = u_ext.shape[1] - (CONV_WIDTH - 1)
    y = w[0] * u_ext[:, 0:t]
    for j in range(1, CONV_WIDTH):
        y = y + w[j] * u_ext[:, j:j + t]
    return y


def _over_query_blocks(fn, qs, q_pos):
    nb = q_pos.shape[0] // Q_BLOCK
    def blk(a):
        return jnp.swapaxes(a.reshape((a.shape[0], nb, Q_BLOCK) + a.shape[2:]), 0, 1)
    out = lax.map(lambda a: fn(a[0], a[1]), (tuple(blk(a) for a in qs), q_pos.reshape(nb, Q_BLOCK)))
    out = jnp.swapaxes(out, 0, 1)
    return out.reshape((out.shape[0], q_pos.shape[0]) + out.shape[3:])


def _diff_attn(q, k, v, q_pos, k_pos, lam, subln_g, lam_init):
    b, t = q.shape[:2]
    s_len = k.shape[1]
    half = HEAD_DIM // 2
    rep = N_HEADS_DIFF // N_KV_DIFF
    qg = q.reshape(b, t, N_KV_DIFF, rep, 2, half)
    kg = k.reshape(b, s_len, N_KV_DIFF, 2, half)
    sc = jnp.einsum('btgrcd,bsgcd->bgrcts', qg, kg).astype(jnp.float32) * (half ** -0.5)
    mask = k_pos[None, :] <= q_pos[:, None]
    p = jax.nn.softmax(jnp.where(mask, sc, -jnp.inf), axis=-1)
    lf = lam.astype(jnp.float32)
    lmb = jnp.exp(jnp.sum(lf[0] * lf[1])) - jnp.exp(jnp.sum(lf[2] * lf[3])) + lam_init
    pd = p[:, :, :, 0] - lmb * p[:, :, :, 1]
    o = jnp.einsum('bgrts,bsgd->btgrd', pd.astype(v.dtype), v).reshape(b, t, N_HEADS_DIFF, HEAD_DIM)
    o = _rmsnorm(o, subln_g) * (1 - lam_init)
    return o.reshape(b, t, D_DIFF)


def _select_keys(qi, wi, ki, q_pos, k_pos, k_top):
    sc = jax.nn.relu(jnp.einsum('bthd,bsd->bths', qi, ki).astype(jnp.float32))
    idx_score = jnp.einsum('bths,bth->bts', sc, wi.astype(jnp.float32) * (N_IDX_HEADS ** -0.5))
    mask = k_pos[None, :] <= q_pos[:, None]
    _, idx = lax.top_k(jnp.where(mask, idx_score, -jnp.inf), k_top)
    return idx


def _sparse_attn(q, k_sel, v_sel, valid):
    b, t = q.shape[:2]
    rep = N_HEADS_DSA // N_KV_DSA
    qg = q.reshape(b, t, N_KV_DSA, rep, HEAD_DIM)
    sc = jnp.einsum('btgrd,btkgd->btgrk', qg, k_sel).astype(jnp.float32) * (HEAD_DIM ** -0.5)
    p = jax.nn.softmax(jnp.where(valid[:, :, None, None, :], sc, -jnp.inf), axis=-1)
    o = jnp.einsum('btgrk,btkgd->btgrd', p.astype(v_sel.dtype), v_sel)
    return o.reshape(b, t, D_DSA)


def _gather_paged_rows(pool, page_table, new_rows, idx, past_len):
    past = jnp.minimum(idx, past_len - 1)
    phys = jax.vmap(lambda pt, pg: pt[pg])(page_table, past // PAGE_SIZE)
    rows_past = pool[phys, past % PAGE_SIZE].astype(new_rows.dtype)
    new_i = jnp.clip(idx - past_len, 0, new_rows.shape[1] - 1)
    rows_new = jax.vmap(lambda r, i: r[i])(new_rows, new_i)
    is_new = (idx >= past_len).reshape(idx.shape + (1,) * (pool.ndim - 2))
    return jnp.where(is_new, rows_new, rows_past)


def _layer(x, c, lam_init, norm_g, w_ada, b_ada, w_in, conv_w, lam, subln_g, w_out,
           conv_prev=None, page_table=None, c_dk=None, c_dv=None, c_sk=None, c_sv=None, c_si=None):
    b, t, _ = x.shape
    parts, gate = _pre_mix(x, c, norm_g, w_ada, b_ada, w_in)
    hx, gb, gc, q_d, k_d, v_d, q_s, k_s, v_s, qi, ki, wi, z = parts
    q_d = q_d.reshape(b, t, N_HEADS_DIFF, HEAD_DIM)
    k_d = k_d.reshape(b, t, N_KV_DIFF, HEAD_DIM)
    v_d = v_d.reshape(b, t, N_KV_DIFF, HEAD_DIM)
    q_s = q_s.reshape(b, t, N_HEADS_DSA, HEAD_DIM)
    k_s = k_s.reshape(b, t, N_KV_DSA, HEAD_DIM)
    v_s = v_s.reshape(b, t, N_KV_DSA, HEAD_DIM)
    qi = qi.reshape(b, t, N_IDX_HEADS, D_IDX)
    u = gc * hx
    if conv_prev is None:
        conv_prev = jnp.zeros((b, CONV_WIDTH - 1, D_CONV), u.dtype)
    u_ext = jnp.concatenate([conv_prev.astype(u.dtype), u], axis=1)
    y_conv = gb * _causal_conv(u_ext, conv_w)
    conv_new = u_ext[:, -(CONV_WIDTH - 1):]
    if page_table is None:
        pos = jnp.arange(t)
        k_top = min(TOPK_MAX, t // 4)

        def diff_blk(qs, qp):
            return _diff_attn(qs[0], k_d, v_d, qp, pos, lam, subln_g, lam_init)

        def dsa_blk(qs, qp):
            qb, qib, wib = qs
            idx = _select_keys(qib, wib, ki, qp, pos, k_top)
            k_sel = jax.vmap(lambda kk, ii: kk[ii])(k_s, idx)
            v_sel = jax.vmap(lambda vv, ii: vv[ii])(v_s, idx)
            return _sparse_attn(qb, k_sel, v_sel, idx <= qp[None, :, None])

        y_diff = _over_query_blocks(diff_blk, (q_d,), pos)
        y_dsa = _over_query_blocks(dsa_blk, (q_s, qi, wi), pos)
    else:
        past_len = page_table.shape[1] * PAGE_SIZE
        q_pos = past_len + jnp.arange(t)
        k_pos = jnp.arange(past_len + t)
        k_top = min(TOPK_MAX, (past_len + t) // 4)
        k_all = jnp.concatenate([c_dk[page_table].reshape(b, past_len, N_KV_DIFF, HEAD_DIM).astype(k_d.dtype), k_d], axis=1)
        v_all = jnp.concatenate([c_dv[page_table].reshape(b, past_len, N_KV_DIFF, HEAD_DIM).astype(v_d.dtype), v_d], axis=1)
        y_diff = _diff_attn(q_d, k_all, v_all, q_pos, k_pos, lam, subln_g, lam_init)
        ki_all = jnp.concatenate([c_si[page_table].reshape(b, past_len, D_IDX).astype(ki.dtype), ki], axis=1)
        idx = _select_keys(qi, wi, ki_all, q_pos, k_pos, k_top)
        k_sel = _gather_paged_rows(c_sk, page_table, k_s, idx, past_len)
        v_sel = _gather_paged_rows(c_sv, page_table, v_s, idx, past_len)
        y_dsa = _sparse_attn(q_s, k_sel, v_sel, idx <= q_pos[None, :, None])
    mix = jnp.concatenate([y_conv, y_diff, y_dsa], axis=-1) * jax.nn.silu(z)
    x = x + gate * (mix @ w_out)
    return x, (k_d, v_d, k_s, v_s, ki, conv_new)


def setup_inputs(seed: int = 0) -> dict:
    key = jax.random.key(seed)
    ks = jax.random.split(key, 24)
    n_pages = PAST_LEN // PAGE_SIZE
    n_pool = (DEC_BATCH * n_pages * 5) // 4
    nrm = jax.random.normal
    page_table = jax.random.permutation(ks[0], n_pool)[:DEC_BATCH * n_pages].reshape(DEC_BATCH, n_pages).astype(jnp.int32)
    return {
        'x_prompt': nrm(ks[1], (BATCH, SEQ, D_MODEL), jnp.float32),
        'x_sample': nrm(ks[2], (DEC_BATCH, DEC_SEQ, D_MODEL), jnp.float32),
        'cache_diff_k': nrm(ks[3], (DEPTH, n_pool, PAGE_SIZE, N_KV_DIFF, HEAD_DIM), jnp.float32),
        'cache_diff_v': nrm(ks[4], (DEPTH, n_pool, PAGE_SIZE, N_KV_DIFF, HEAD_DIM), jnp.float32),
        'cache_dsa_k': nrm(ks[5], (DEPTH, n_pool, PAGE_SIZE, N_KV_DSA, HEAD_DIM), jnp.float32),
        'cache_dsa_v': nrm(ks[6], (DEPTH, n_pool, PAGE_SIZE, N_KV_DSA, HEAD_DIM), jnp.float32),
        'cache_dsa_kidx': nrm(ks[7], (DEPTH, n_pool, PAGE_SIZE, D_IDX), jnp.float32),
        'state_conv': nrm(ks[8], (DEPTH, DEC_BATCH, CONV_WIDTH - 1, D_CONV), jnp.float32),
        'page_table': page_table,
        'c_prompt': nrm(ks[9], (BATCH, D_MODEL), jnp.float32),
        'c_sample': nrm(ks[10], (DEC_BATCH, D_MODEL), jnp.float32),
        'norm_g': 1.0 + 0.02 * nrm(ks[11], (DEPTH, D_MODEL), jnp.float32),
        'w_ada': nrm(ks[12], (DEPTH, D_MODEL, 3 * D_MODEL), jnp.float32) * D_MODEL ** -0.5,
        'b_ada': 0.02 * nrm(ks[13], (DEPTH, 3 * D_MODEL), jnp.float32),
        'w_in': nrm(ks[14], (DEPTH, D_MODEL, D_IN), jnp.float32) * D_MODEL ** -0.5,
        'conv_w': nrm(ks[15], (DEPTH, CONV_WIDTH, D_CONV), jnp.float32) * CONV_WIDTH ** -0.5,
        'diff_lam': 0.1 * nrm(ks[16], (DEPTH, 4, HEAD_DIM // 2), jnp.float32),
        'diff_subln_g': 1.0 + 0.02 * nrm(ks[17], (DEPTH, HEAD_DIM), jnp.float32),
        'w_out': nrm(ks[18], (DEPTH, D_MIX, D_MODEL), jnp.float32) * D_MIX ** -0.5,
        'final_g': 1.0 + 0.02 * nrm(ks[19], (D_MODEL,), jnp.float32),
    }


def reference(x_prompt, x_sample, cache_diff_k, cache_diff_v, cache_dsa_k, cache_dsa_v, cache_dsa_kidx,
              state_conv, page_table, c_prompt, c_sample, norm_g, w_ada, b_ada, w_in, conv_w,
              diff_lam, diff_subln_g, w_out, final_g):
    xp, xs = x_prompt, x_sample
    p_rows, s_rows = [], []
    for l in range(DEPTH):
        lam_init = LAMBDA_INIT_BASE - LAMBDA_INIT_AMP * math.exp(-LAMBDA_INIT_RATE * l)
        w_l = (norm_g[l], w_ada[l], b_ada[l], w_in[l], conv_w[l], diff_lam[l], diff_subln_g[l], w_out[l])
        xp, rp = _layer(xp, c_prompt, lam_init, *w_l)
        xs, rs = _layer(xs, c_sample, lam_init, *w_l, conv_prev=state_conv[l], page_table=page_table,
                        c_dk=cache_diff_k[l], c_dv=cache_diff_v[l], c_sk=cache_dsa_k[l],
                        c_sv=cache_dsa_v[l], c_si=cache_dsa_kidx[l])
        p_rows.append(rp)
        s_rows.append(rs)
    y_prompt = _rmsnorm(xp, final_g)
    y_sample = _rmsnorm(xs, final_g)
    return (y_prompt, y_sample,
            jnp.stack([r[0] for r in p_rows]), jnp.stack([r[1] for r in p_rows]),
            jnp.stack([r[2] for r in p_rows]), jnp.stack([r[3] for r in p_rows]),
            jnp.stack([r[4] for r in p_rows]), jnp.stack([r[5] for r in p_rows]),
            jnp.stack([r[0] for r in s_rows]), jnp.stack([r[1] for r in s_rows]),
            jnp.stack([r[2] for r in s_rows]), jnp.stack([r[3] for r in s_rows]),
            jnp.stack([r[4] for r in s_rows]), jnp.stack([r[5] for r in s_rows]))
```

```python
import functools
import math
from typing import NamedTuple

import jax
import jax.numpy as jnp
from jax import lax
from jax.experimental import pallas as pl
from jax.experimental.pallas import tpu as pltpu

F32 = jnp.float32
BF16 = jnp.bfloat16

HEAD_DIM = 128
D_IDX = 64
N_IDX_HEADS = 16
CONV_WIDTH = 3
TOPK_MAX = 256
PAGE_SIZE = 128
RMS_EPS = 1e-6
LAMBDA_INIT_BASE = 0.8
LAMBDA_INIT_AMP = 0.6
LAMBDA_INIT_RATE = 0.3

LANES = 128
SUBLANES = 8
MIB = 2 ** 20

NEG_INF = float("-inf")
INT_MIN = -(2 ** 31)
KEY_NEG_INF = INT_MIN + 0x7FFFFF


class Cfg(NamedTuple):
    d_model: int
    d_conv: int
    n_kv_diff: int
    rep_diff: int
    d_diff: int
    n_kv_dsa: int
    rep_dsa: int
    d_dsa: int
    n_a: int
    n_b: int
    off_b: dict


def _make_cfg(d_model, n_kv_diff, n_kv_dsa):
    d_mix = d_model
    d_conv = d_mix // 4
    d_diff = 3 * d_mix // 8
    d_dsa = 3 * d_mix // 8
    n_a = d_diff + d_dsa + N_IDX_HEADS * D_IDX + d_mix
    off, pos = {}, 0
    for name, width in (("hx", d_conv), ("gb", d_conv), ("gc", d_conv),
                        ("k_d", n_kv_diff * HEAD_DIM), ("v_d", n_kv_diff * HEAD_DIM),
                        ("k_s", n_kv_dsa * HEAD_DIM), ("v_s", n_kv_dsa * HEAD_DIM),
                        ("kiwi", LANES)):
        off[name] = pos
        pos += width
    n_b = -(-pos // 512) * 512
    return Cfg(d_model, d_conv, n_kv_diff, d_diff // HEAD_DIM // n_kv_diff, d_diff,
               n_kv_dsa, d_dsa // HEAD_DIM // n_kv_dsa, d_dsa, n_a, n_b, off)


def _params(semantics, vmem_mib):
    return pltpu.CompilerParams(dimension_semantics=semantics, vmem_limit_bytes=vmem_mib * MIB)


def _dot_nt(a, b):
    return lax.dot_general(a, b, (((1,), (1,)), ((), ())), preferred_element_type=F32)


def _mod_kernel(c_ref, w_ref, b_ref, o_ref):
    c = c_ref[...]
    a = (c * jax.nn.sigmoid(c)).astype(BF16)
    o_ref[...] = jnp.dot(a, w_ref[...].astype(BF16), preferred_element_type=F32) + b_ref[...]


def _mod_call(c_all, w_ada, b_ada):
    depth, d, n3 = w_ada.shape
    rows = c_all.shape[0]
    tn = 512
    return pl.pallas_call(
        _mod_kernel,
        grid=(depth, n3 // tn),
        in_specs=[pl.BlockSpec((rows, d), lambda l, j: (0, 0)),
                  pl.BlockSpec((None, d, tn), lambda l, j: (l, 0, j)),
                  pl.BlockSpec((None, 1, tn), lambda l, j: (l, 0, j))],
        out_specs=pl.BlockSpec((None, rows, tn), lambda l, j: (l, 0, j)),
        out_shape=jax.ShapeDtypeStruct((depth, rows, n3), F32),
        compiler_params=_params(("arbitrary", "arbitrary"), 40),
        name="adaln_mod",
    )(c_all, w_ada, b_ada.reshape(depth, 1, n3))


def _rms(x):
    return x * lax.rsqrt(jnp.mean(x * x, axis=-1, keepdims=True) + RMS_EPS)


def _norm_mod_kernel(x_ref, g_ref, mod_ref, o_ref):
    y = _rms(x_ref[...]) * g_ref[...]
    o_ref[...] = (y * (1.0 + mod_ref[1:2, :]) + mod_ref[0:1, :]).astype(o_ref.dtype)


def _norm_mod_call(x2d, g, mod3, rows_per_batch, tm, out_dtype):
    m, d = x2d.shape
    per = rows_per_batch // tm
    return pl.pallas_call(
        _norm_mod_kernel,
        grid=(m // tm,),
        in_specs=[pl.BlockSpec((tm, d), lambda i: (i, 0)),
                  pl.BlockSpec((1, d), lambda i: (0, 0)),
                  pl.BlockSpec((None, 3, d), lambda i: (i // per, 0, 0))],
        out_specs=pl.BlockSpec((tm, d), lambda i: (i, 0)),
        out_shape=jax.ShapeDtypeStruct((m, d), out_dtype),
        compiler_params=_params(("arbitrary",), 40),
        name="norm_mod",
    )(x2d, g.reshape(1, d), mod3)


def _rmsnorm_kernel(x_ref, g_ref, o_ref):
    o_ref[...] = _rms(x_ref[...]) * g_ref[...]


def _rmsnorm_call(x2d, g, tm):
    m, d = x2d.shape
    return pl.pallas_call(
        _rmsnorm_kernel,
        grid=(m // tm,),
        in_specs=[pl.BlockSpec((tm, d), lambda i: (i, 0)),
                  pl.BlockSpec((1, d), lambda i: (0, 0))],
        out_specs=pl.BlockSpec((tm, d), lambda i: (i, 0)),
        out_shape=jax.ShapeDtypeStruct((m, d), F32),
        compiler_params=_params(("arbitrary",), 40),
        name="final_norm",
    )(x2d, g.reshape(1, d))


def _matmul_kernel(x_ref, w_ref, o_ref):
    o_ref[...] = jnp.dot(x_ref[...], w_ref[...], preferred_element_type=F32).astype(o_ref.dtype)


def _matmul_call(x16, w16, col0, n, out_dtype, tm, tn=512):
    m, k = x16.shape
    blk0 = col0 // tn
    return pl.pallas_call(
        _matmul_kernel,
        grid=(m // tm, n // tn),
        in_specs=[pl.BlockSpec((tm, k), lambda i, j: (i, 0)),
                  pl.BlockSpec((k, tn), lambda i, j: (0, blk0 + j))],
        out_specs=pl.BlockSpec((tm, tn), lambda i, j: (i, j)),
        out_shape=jax.ShapeDtypeStruct((m, n), out_dtype),
        compiler_params=_params(("arbitrary", "arbitrary"), 48),
        name="in_proj",
    )(x16, w16)


def _half_masked(q, rows):
    lane = lax.broadcasted_iota(jnp.int32, (rows, HEAD_DIM), 1)
    zero = jnp.zeros_like(q)
    return jnp.where(lane < HEAD_DIM // 2, q, zero), jnp.where(lane >= HEAD_DIM // 2, q, zero)


def _lambda_full(lam_ref, lam_init):
    lam = lam_ref[...]
    a = jnp.sum(lam[0:1, :] * lam[1:2, :], axis=-1, keepdims=True)
    b = jnp.sum(lam[2:3, :] * lam[3:4, :], axis=-1, keepdims=True)
    return jnp.exp(a) - jnp.exp(b) + lam_init


def _diff_head_out(o0, o1, lmb, sg, lam_init):
    o = o0 - lmb * o1
    return _rms(o) * sg * (1.0 - lam_init)


def _online_update(m_sc, l_sc, acc_sc, rows, s, v16):
    m_prev = m_sc[rows]
    m_new = jnp.maximum(m_prev, jnp.max(s, axis=-1, keepdims=True))
    m_safe = jnp.where(m_new == NEG_INF, 0.0, m_new)
    alpha = jnp.exp(m_prev - m_safe)
    p = jnp.exp(s - m_safe)
    l_sc[rows] = alpha * l_sc[rows] + jnp.sum(p, axis=-1, keepdims=True)
    acc_sc[rows] = alpha * acc_sc[rows] + jnp.dot(p.astype(BF16), v16, preferred_element_type=F32)
    m_sc[rows] = m_new


def _diff_prompt_kernel(q_ref, k_ref, v_ref, lam_ref, sg_ref, o_ref, m_sc, l_sc, acc_sc,
                        *, rep, tq, lam_init):
    i = pl.program_id(2)
    n_stack = 2 * rep
    q = q_ref[...]
    pieces = []
    for r in range(rep):
        qr = q[:, r * HEAD_DIM:(r + 1) * HEAD_DIM] * ((HEAD_DIM // 2) ** -0.5)
        pieces.extend(_half_masked(qr, tq))
    qs = jnp.concatenate(pieces, axis=0)
    m_sc[...] = jnp.full_like(m_sc, NEG_INF)
    l_sc[...] = jnp.zeros_like(l_sc)
    acc_sc[...] = jnp.zeros_like(acc_sc)
    allrows = slice(None)

    def kv(j):
        start = pl.multiple_of(j * tq, tq)
        return k_ref[pl.ds(start, tq), :], v_ref[pl.ds(start, tq), :]

    def full_block(j, carry):
        k, v = kv(j)
        _online_update(m_sc, l_sc, acc_sc, allrows, _dot_nt(qs, k), v)
        return carry

    lax.fori_loop(0, i, full_block, 0)
    k, v = kv(i)
    s = _dot_nt(qs, k).reshape(n_stack, tq, tq)
    row = lax.broadcasted_iota(jnp.int32, (tq, tq), 0)
    col = lax.broadcasted_iota(jnp.int32, (tq, tq), 1)
    s = jnp.where((col <= row)[None], s, NEG_INF).reshape(n_stack * tq, tq)
    _online_update(m_sc, l_sc, acc_sc, allrows, s, v)

    lmb = _lambda_full(lam_ref, lam_init)
    o_all = acc_sc[...] / l_sc[...]
    for r in range(rep):
        o0 = o_all[(2 * r) * tq:(2 * r + 1) * tq]
        o1 = o_all[(2 * r + 1) * tq:(2 * r + 2) * tq]
        y = _diff_head_out(o0, o1, lmb, sg_ref[...], lam_init)
        o_ref[:, r * HEAD_DIM:(r + 1) * HEAD_DIM] = y.astype(BF16)


def _diff_prompt_call(a_p, kd16, vd16, lam, sg, cfg, nb, t, lam_init, tq=256):
    m = a_p.shape[0]
    nq = t // tq
    g_w = cfg.rep_diff * HEAD_DIM
    rows = 2 * cfg.rep_diff * tq
    kern = functools.partial(_diff_prompt_kernel, rep=cfg.rep_diff, tq=tq, lam_init=lam_init)
    return pl.pallas_call(
        kern,
        grid=(nb, cfg.n_kv_diff, nq),
        in_specs=[pl.BlockSpec((tq, g_w), lambda b, g, i: (b * nq + i, g)),
                  pl.BlockSpec((t, HEAD_DIM), lambda b, g, i: (b, g)),
                  pl.BlockSpec((t, HEAD_DIM), lambda b, g, i: (b, g)),
                  pl.BlockSpec(lam.shape, lambda b, g, i: (0, 0)),
                  pl.BlockSpec((1, HEAD_DIM), lambda b, g, i: (0, 0))],
        out_specs=pl.BlockSpec((tq, g_w), lambda b, g, i: (b * nq + i, g)),
        out_shape=jax.ShapeDtypeStruct((m, cfg.d_diff), BF16),
        scratch_shapes=[pltpu.VMEM((rows, 1), F32), pltpu.VMEM((rows, 1), F32),
                        pltpu.VMEM((rows, HEAD_DIM), F32)],
        compiler_params=_params(("arbitrary", "arbitrary", "arbitrary"), 40),
        name="diff_attn_prompt",
    )(a_p, kd16, vd16, lam, sg.reshape(1, HEAD_DIM))


def _monotone_key(x):
    bits = lax.bitcast_convert_type(x, jnp.int32)
    key = bits ^ ((bits >> 31) & jnp.int32(0x7FFFFFFF))
    return jnp.where(bits == jnp.int32(INT_MIN), 0, key)


def _count(mask):
    return jnp.sum(jnp.where(mask, 1.0, 0.0), axis=-1, keepdims=True)


def _topk_mask(score, k_top):
    r, w = score.shape
    key = _monotone_key(score)
    kf = float(k_top)

    def value_bit(it, t):
        cand = t + (jnp.int32(1) << (31 - it))
        return jnp.where(_count(key >= cand) >= kf, cand, t)

    t = lax.fori_loop(0, 32, value_bit, jnp.full((r, 1), INT_MIN, jnp.int32))
    gt = key > t
    eq = key == t
    n_gt = _count(gt)
    need = kf - n_gt
    col = lax.broadcasted_iota(jnp.int32, (r, w), 1)
    ambiguous = (n_gt + _count(eq) > kf) & (t > KEY_NEG_INF)
    n_bits = w.bit_length()

    def tie_search(_):
        def index_bit(it, j):
            cand = j + (jnp.int32(1) << (n_bits - 1 - it))
            return jnp.where(_count(eq & (col < cand)) < need, cand, j)
        return lax.fori_loop(0, n_bits, index_bit, jnp.zeros((r, 1), jnp.int32))

    any_amb = jnp.max(jnp.where(ambiguous, 1.0, 0.0)) > 0.5
    j = lax.cond(any_amb, tie_search, lambda _: jnp.full((r, 1), w, jnp.int32), 0)
    return gt | (eq & (col <= j))


def _selection_bias(score, k_top):
    sel = _topk_mask(score, k_top) & (score > NEG_INF)
    return jnp.where(sel, 0.0, NEG_INF)


def _dsa_prompt_kernel(qi_ref, kiwi_ref, qs_ref, kiki_ref, ks_ref, vs_ref, o_ref,
                       *, tq, w, k_top, n_kv, rep):
    i = pl.program_id(1)
    wi = kiwi_ref[...][:, D_IDX:D_IDX + N_IDX_HEADS] * (N_IDX_HEADS ** -0.5)
    kiki = kiki_ref[...]
    idx = jnp.zeros((tq, w), F32)
    for j in range(N_IDX_HEADS // 2):
        lo, hi = _half_masked(qi_ref[:, j * LANES:(j + 1) * LANES], tq)
        sc = jnp.maximum(_dot_nt(jnp.concatenate([lo, hi], axis=0), kiki), 0.0)
        idx = idx + sc[:tq] * wi[:, 2 * j:2 * j + 1] + sc[tq:] * wi[:, 2 * j + 1:2 * j + 2]
    row = lax.broadcasted_iota(jnp.int32, (tq, w), 0) + i * tq
    col = lax.broadcasted_iota(jnp.int32, (tq, w), 1)
    bias = _selection_bias(jnp.where(col <= row, idx, NEG_INF), k_top)

    scale = HEAD_DIM ** -0.5
    for g in range(n_kv):
        heads = [qs_ref[:, (g * rep + r) * HEAD_DIM:(g * rep + r + 1) * HEAD_DIM] for r in range(rep)]
        s = _dot_nt(jnp.concatenate(heads, axis=0), ks_ref[:, g * HEAD_DIM:(g + 1) * HEAD_DIM]) * scale
        s = s.reshape(rep, tq, w) + bias[None]
        p = jnp.exp(s - jnp.max(s, axis=-1, keepdims=True))
        l = jnp.sum(p, axis=-1, keepdims=True).reshape(rep * tq, 1)
        o = jnp.dot(p.reshape(rep * tq, w).astype(BF16), vs_ref[:, g * HEAD_DIM:(g + 1) * HEAD_DIM],
                    preferred_element_type=F32) / l
        for r in range(rep):
            o_ref[:, (g * rep + r) * HEAD_DIM:(g * rep + r + 1) * HEAD_DIM] = o[r * tq:(r + 1) * tq].astype(BF16)


def _dsa_prompt_call(a_p, b_p, kiki16, ks16, vs16, cfg, nb, t, tq=128):
    m = a_p.shape[0]
    nq = t // tq
    k_top = min(TOPK_MAX, t // 4)
    qi_blk = (cfg.d_diff + cfg.d_dsa) // (N_IDX_HEADS * D_IDX)
    kiwi_blk = cfg.off_b["kiwi"] // LANES
    kvw = cfg.n_kv_dsa * HEAD_DIM
    kern = functools.partial(_dsa_prompt_kernel, tq=tq, w=t, k_top=k_top, n_kv=cfg.n_kv_dsa, rep=cfg.rep_dsa)
    return pl.pallas_call(
        kern,
        grid=(nb, nq),
        in_specs=[pl.BlockSpec((tq, N_IDX_HEADS * D_IDX), lambda b, i: (b * nq + i, qi_blk)),
                  pl.BlockSpec((tq, LANES), lambda b, i: (b * nq + i, kiwi_blk)),
                  pl.BlockSpec((tq, cfg.d_dsa), lambda b, i: (b * nq + i, 1)),
                  pl.BlockSpec((t, LANES), lambda b, i: (b, 0)),
                  pl.BlockSpec((t, kvw), lambda b, i: (b, 0)),
                  pl.BlockSpec((t, kvw), lambda b, i: (b, 0))],
        out_specs=pl.BlockSpec((tq, cfg.d_dsa), lambda b, i: (b * nq + i, 0)),
        out_shape=jax.ShapeDtypeStruct((m, cfg.d_dsa), BF16),
        compiler_params=_params(("arbitrary", "arbitrary"), 56),
        name="dsa_prompt",
    )(a_p, b_p, a_p, kiki16, ks16, vs16)


def _mix_out_kernel(x_ref, hx_ref, gb_ref, gc_ref, ha_ref, hb_ref, cw_ref, yd_ref, ys_ref, z_ref,
                    mod_ref, wo_ref, o_ref, ut_ref, mix_sc, *, tm, seq_tiles, d_conv, d_diff):
    i = pl.program_id(0)

    @pl.when(pl.program_id(1) == 0)
    def _():
        u = gc_ref[...] * hx_ref[...]
        uh = ha_ref[...] * hb_ref[...]
        if seq_tiles:
            uh = jnp.where(i % seq_tiles == 0, 0.0, uh)
        row = lax.broadcasted_iota(jnp.int32, (tm, d_conv), 0)
        u1 = jnp.where(row == 0, uh[7:8], pltpu.roll(u, 1, 0))
        u2 = jnp.where(row == 0, uh[6:7], jnp.where(row == 1, uh[7:8], pltpu.roll(u, 2, 0)))
        cw = cw_ref[...]
        yc = gb_ref[...] * (cw[0:1] * u2 + cw[1:2] * u1 + cw[2:3] * u)
        ut_ref[...] = u[tm - SUBLANES:tm]

        def gated(y, c0, c1):
            zf = z_ref[:, c0:c1].astype(F32)
            mix_sc[:, c0:c1] = (y * (zf * jax.nn.sigmoid(zf))).astype(BF16)

        gated(yc, 0, d_conv)
        gated(yd_ref[...].astype(F32), d_conv, d_conv + d_diff)
        gated(ys_ref[...].astype(F32), d_conv + d_diff, mix_sc.shape[1])

    o_ref[...] = x_ref[...] + mod_ref[2:3, :] * jnp.dot(mix_sc[...], wo_ref[...], preferred_element_type=F32)


def _mix_out_call(x2d, b_x, halo_a, halo_b, halo_map_a, halo_map_b, conv_w, y_diff, y_dsa, a_x, mod3,
                  wo16, cfg, rows_per_batch, tm, seq_tiles, tn=512):
    m, d = x2d.shape
    dc = cfg.d_conv
    per = rows_per_batch // tm
    z_blk = (cfg.n_a - d) // d
    kern = functools.partial(_mix_out_kernel, tm=tm, seq_tiles=seq_tiles, d_conv=dc, d_diff=cfg.d_diff)
    return pl.pallas_call(
        kern,
        grid=(m // tm, d // tn),
        in_specs=[pl.BlockSpec((tm, tn), lambda i, j: (i, j)),
                  pl.BlockSpec((tm, dc), lambda i, j: (i, 0)),
                  pl.BlockSpec((tm, dc), lambda i, j: (i, 1)),
                  pl.BlockSpec((tm, dc), lambda i, j: (i, 2)),
                  pl.BlockSpec((SUBLANES, dc), halo_map_a),
                  pl.BlockSpec((SUBLANES, dc), halo_map_b),
                  pl.BlockSpec(conv_w.shape, lambda i, j: (0, 0)),
                  pl.BlockSpec((tm, cfg.d_diff), lambda i, j: (i, 0)),
                  pl.BlockSpec((tm, cfg.d_dsa), lambda i, j: (i, 0)),
                  pl.BlockSpec((tm, d), lambda i, j: (i, z_blk)),
                  pl.BlockSpec((None, 3, tn), lambda i, j: (i // per, 0, j)),
                  pl.BlockSpec((d, tn), lambda i, j: (0, j))],
        out_specs=[pl.BlockSpec((tm, tn), lambda i, j: (i, j)),
                   pl.BlockSpec((None, SUBLANES, dc), lambda i, j: (i, 0, 0))],
        out_shape=[jax.ShapeDtypeStruct((m, d), F32),
                   jax.ShapeDtypeStruct((m // tm, SUBLANES, dc), F32)],
        scratch_shapes=[pltpu.VMEM((tm, d), BF16)],
        compiler_params=_params(("arbitrary", "arbitrary"), 56),
        name="mix_out",
    )(x2d, b_x, b_x, b_x, halo_a, halo_b, conv_w, y_diff, y_dsa, a_x, mod3, wo16)


def _page_specs(layer, pps, width):
    def spec(p):
        return pl.BlockSpec((None, None, PAGE_SIZE, width),
                            lambda b, s, pt: (layer, pt[b, s * pps + p], 0, 0))
    return [spec(p) for p in range(pps)]


def _cat_pages(refs, c0, c1):
    return jnp.concatenate([r[:, c0:c1] for r in refs], axis=0).astype(BF16)


def _dec_diff_kernel(pt_ref, q_ref, *refs, pps, n_kv, rep, t_new, lam_init):
    kp, vp = refs[:pps], refs[pps:2 * pps]
    knew_ref, vnew_ref, lam_ref, sg_ref, o_ref, qs_sc, m_sc, l_sc, acc_sc = refs[2 * pps:]
    step = pl.program_id(1)
    rg = 2 * rep * SUBLANES

    @pl.when(step == 0)
    def _():
        for h in range(n_kv * rep):
            qh = q_ref[:, h * HEAD_DIM:(h + 1) * HEAD_DIM] * ((HEAD_DIM // 2) ** -0.5)
            lo, hi = _half_masked(qh, SUBLANES)
            qs_sc[2 * h * SUBLANES:(2 * h + 1) * SUBLANES] = lo
            qs_sc[(2 * h + 1) * SUBLANES:(2 * h + 2) * SUBLANES] = hi
        m_sc[...] = jnp.full_like(m_sc, NEG_INF)
        l_sc[...] = jnp.zeros_like(l_sc)
        acc_sc[...] = jnp.zeros_like(acc_sc)

    for g in range(n_kv):
        rows = slice(g * rg, (g + 1) * rg)
        c0, c1 = g * HEAD_DIM, (g + 1) * HEAD_DIM
        _online_update(m_sc, l_sc, acc_sc, rows, _dot_nt(qs_sc[rows].astype(BF16), _cat_pages(kp, c0, c1)),
                       _cat_pages(vp, c0, c1))

    @pl.when(step == pl.num_programs(1) - 1)
    def _():
        tok = lax.broadcasted_iota(jnp.int32, (rg, LANES), 0) % SUBLANES
        col = lax.broadcasted_iota(jnp.int32, (rg, LANES), 1)
        valid = (col <= tok) & (col < t_new)
        for g in range(n_kv):
            rows = slice(g * rg, (g + 1) * rg)
            c0, c1 = g * HEAD_DIM, (g + 1) * HEAD_DIM
            s = jnp.where(valid, _dot_nt(qs_sc[rows].astype(BF16), knew_ref[:, c0:c1]), NEG_INF)
            _online_update(m_sc, l_sc, acc_sc, rows, s, vnew_ref[:, c0:c1])
        lmb = _lambda_full(lam_ref, lam_init)
        o_all = acc_sc[...] / l_sc[...]
        for h in range(n_kv * rep):
            o0 = o_all[2 * h * SUBLANES:(2 * h + 1) * SUBLANES]
            o1 = o_all[(2 * h + 1) * SUBLANES:(2 * h + 2) * SUBLANES]
            o_ref[:, h * HEAD_DIM:(h + 1) * HEAD_DIM] = _diff_head_out(o0, o1, lmb, sg_ref[...], lam_init)


def _dec_diff_call(page_table, a_s, cache_k, cache_v, knew16, vnew16, lam, sg, cfg, layer, t_new,
                   lam_init, pps=8):
    nb, n_pages = page_table.shape
    kvw = cfg.n_kv_diff * HEAD_DIM
    rows = cfg.n_kv_diff * 2 * cfg.rep_diff * SUBLANES
    kern = functools.partial(_dec_diff_kernel, pps=pps, n_kv=cfg.n_kv_diff, rep=cfg.rep_diff,
                             t_new=t_new, lam_init=lam_init)
    grid_spec = pltpu.PrefetchScalarGridSpec(
        num_scalar_prefetch=1,
        grid=(nb, n_pages // pps),
        in_specs=[pl.BlockSpec((SUBLANES, cfg.d_diff), lambda b, s, pt: (b, 0))]
        + _page_specs(layer, pps, kvw) + _page_specs(layer, pps, kvw)
        + [pl.BlockSpec((None, PAGE_SIZE, kvw), lambda b, s, pt: (b, 0, 0)),
           pl.BlockSpec((None, PAGE_SIZE, kvw), lambda b, s, pt: (b, 0, 0)),
           pl.BlockSpec(lam.shape, lambda b, s, pt: (0, 0)),
           pl.BlockSpec((1, HEAD_DIM), lambda b, s, pt: (0, 0))],
        out_specs=pl.BlockSpec((SUBLANES, cfg.d_diff), lambda b, s, pt: (b, 0)),
        scratch_shapes=[pltpu.VMEM((rows, HEAD_DIM), F32), pltpu.VMEM((rows, 1), F32),
                        pltpu.VMEM((rows, 1), F32), pltpu.VMEM((rows, HEAD_DIM), F32)])
    return pl.pallas_call(
        kern,
        grid_spec=grid_spec,
        out_shape=jax.ShapeDtypeStruct((nb * SUBLANES, cfg.d_diff), F32),
        compiler_params=_params(("arbitrary", "arbitrary"), 40),
        name="diff_attn_sample",
    )(page_table, a_s, *([cache_k] * pps), *([cache_v] * pps), knew16, vnew16, lam, sg.reshape(1, HEAD_DIM))


def _dec_index_kernel(pt_ref, q_ref, w_ref, *refs, pps, k_top, t_new, past_len):
    kp = refs[:pps]
    knew_ref, bias_ref, sc_sc = refs[pps:]
    step = pl.program_id(1)
    span = pps * PAGE_SIZE

    def idx_score(k16):
        s = jnp.maximum(_dot_nt(q_ref[...], k16), 0.0) * (w_ref[...] * (N_IDX_HEADS ** -0.5))
        return jnp.sum(s.reshape(N_IDX_HEADS, SUBLANES, s.shape[-1]), axis=0)

    off = pl.multiple_of(step * span, span)
    sc_sc[:, pl.ds(off, span)] = idx_score(jnp.concatenate([r[...] for r in kp], axis=0).astype(BF16))

    @pl.when(step == pl.num_programs(1) - 1)
    def _():
        tok = lax.broadcasted_iota(jnp.int32, (SUBLANES, LANES), 0)
        col = lax.broadcasted_iota(jnp.int32, (SUBLANES, LANES), 1)
        valid = (col <= tok) & (col < t_new)
        sc_sc[:, past_len:past_len + LANES] = jnp.where(valid, idx_score(knew_ref[...]), NEG_INF)
        bias_ref[...] = _selection_bias(sc_sc[...], k_top)


def _dec_index_call(page_table, qi_rows, wi_col, cache_kidx, kinew16, layer, t_new, pps=16):
    nb, n_pages = page_table.shape
    past_len = n_pages * PAGE_SIZE
    k_top = min(TOPK_MAX, (past_len + t_new) // 4)
    w = past_len + LANES
    rows = N_IDX_HEADS * SUBLANES
    kern = functools.partial(_dec_index_kernel, pps=pps, k_top=k_top, t_new=t_new, past_len=past_len)
    grid_spec = pltpu.PrefetchScalarGridSpec(
        num_scalar_prefetch=1,
        grid=(nb, n_pages // pps),
        in_specs=[pl.BlockSpec((None, rows, D_IDX), lambda b, s, pt: (b, 0, 0)),
                  pl.BlockSpec((None, rows, 1), lambda b, s, pt: (b, 0, 0))]
        + _page_specs(layer, pps, D_IDX)
        + [pl.BlockSpec((None, PAGE_SIZE, D_IDX), lambda b, s, pt: (b, 0, 0))],
        out_specs=pl.BlockSpec((None, SUBLANES, w), lambda b, s, pt: (b, 0, 0)),
        scratch_shapes=[pltpu.VMEM((SUBLANES, w), F32)])
    return pl.pallas_call(
        kern,
        grid_spec=grid_spec,
        out_shape=jax.ShapeDtypeStruct((nb, SUBLANES, w), F32),
        compiler_params=_params(("arbitrary", "arbitrary"), 40),
        name="dsa_index_sample",
    )(page_table, qi_rows, wi_col, *([cache_kidx] * pps), kinew16)


def _dec_dsa_kernel(pt_ref, q_ref, bias_ref, bnew_ref, *refs, pps, n_kv, rep):
    kp, vp = refs[:pps], refs[pps:2 * pps]
    knew_ref, vnew_ref, o_ref, qs_sc, m_sc, l_sc, acc_sc = refs[2 * pps:]
    step = pl.program_id(1)
    rg = rep * SUBLANES
    scale = HEAD_DIM ** -0.5

    @pl.when(step == 0)
    def _():
        for h in range(n_kv * rep):
            qs_sc[h * SUBLANES:(h + 1) * SUBLANES] = q_ref[:, h * HEAD_DIM:(h + 1) * HEAD_DIM]
        m_sc[...] = jnp.full_like(m_sc, NEG_INF)
        l_sc[...] = jnp.zeros_like(l_sc)
        acc_sc[...] = jnp.zeros_like(acc_sc)

    def attend(g, k16, v16, bias):
        rows = slice(g * rg, (g + 1) * rg)
        s = _dot_nt(qs_sc[rows].astype(BF16), k16) * scale
        n = s.shape[-1]
        s = (s.reshape(rep, SUBLANES, n) + bias[None]).reshape(rg, n)
        _online_update(m_sc, l_sc, acc_sc, rows, s, v16)

    for g in range(n_kv):
        c0, c1 = g * HEAD_DIM, (g + 1) * HEAD_DIM
        attend(g, _cat_pages(kp, c0, c1), _cat_pages(vp, c0, c1), bias_ref[...])

    @pl.when(step == pl.num_programs(1) - 1)
    def _():
        for g in range(n_kv):
            c0, c1 = g * HEAD_DIM, (g + 1) * HEAD_DIM
            attend(g, knew_ref[:, c0:c1], vnew_ref[:, c0:c1], bnew_ref[...])
        o_all = acc_sc[...] / l_sc[...]
        for h in range(n_kv * rep):
            o_ref[:, h * HEAD_DIM:(h + 1) * HEAD_DIM] = o_all[h * SUBLANES:(h + 1) * SUBLANES]


def _dec_dsa_call(page_table, a_s, bias, cache_k, cache_v, knew16, vnew16, cfg, layer, pps=8):
    nb, n_pages = page_table.shape
    kvw = cfg.n_kv_dsa * HEAD_DIM
    rows = cfg.n_kv_dsa * cfg.rep_dsa * SUBLANES
    span = pps * PAGE_SIZE
    new_blk = n_pages * PAGE_SIZE // LANES
    kern = functools.partial(_dec_dsa_kernel, pps=pps, n_kv=cfg.n_kv_dsa, rep=cfg.rep_dsa)
    grid_spec = pltpu.PrefetchScalarGridSpec(
        num_scalar_prefetch=1,
        grid=(nb, n_pages // pps),
        in_specs=[pl.BlockSpec((SUBLANES, cfg.d_dsa), lambda b, s, pt: (b, 1)),
                  pl.BlockSpec((None, SUBLANES, span), lambda b, s, pt: (b, 0, s)),
                  pl.BlockSpec((None, SUBLANES, LANES), lambda b, s, pt: (b, 0, new_blk))]
        + _page_specs(layer, pps, kvw) + _page_specs(layer, pps, kvw)
        + [pl.BlockSpec((None, PAGE_SIZE, kvw), lambda b, s, pt: (b, 0, 0)),
           pl.BlockSpec((None, PAGE_SIZE, kvw), lambda b, s, pt: (b, 0, 0))],
        out_specs=pl.BlockSpec((SUBLANES, cfg.d_dsa), lambda b, s, pt: (b, 0)),
        scratch_shapes=[pltpu.VMEM((rows, HEAD_DIM), F32), pltpu.VMEM((rows, 1), F32),
                        pltpu.VMEM((rows, 1), F32), pltpu.VMEM((rows, HEAD_DIM), F32)])
    return pl.pallas_call(
        kern,
        grid_spec=grid_spec,
        out_shape=jax.ShapeDtypeStruct((nb * SUBLANES, cfg.d_dsa), F32),
        compiler_params=_params(("arbitrary", "arbitrary"), 40),
        name="dsa_attn_sample",
    )(page_table, a_s, bias, bias, *([cache_k] * pps), *([cache_v] * pps), knew16, vnew16)


def _pack_w_in(w, cfg):
    dc, dd, ds = cfg.d_conv, cfg.d_diff, cfg.d_dsa
    kd, ks = cfg.n_kv_diff * HEAD_DIM, cfg.n_kv_dsa * HEAD_DIM
    widths = (dc, dc, dc, dd, kd, kd, ds, ks, ks, N_IDX_HEADS * D_IDX, D_IDX, N_IDX_HEADS, cfg.d_model)
    cuts = [0]
    for wd in widths:
        cuts.append(cuts[-1] + wd)
    assert cuts[-1] == w.shape[1]
    hx, gb, gc, q_d, k_d, v_d, q_s, k_s, v_s, qi, ki, wi, z = [w[:, cuts[n]:cuts[n + 1]] for n in range(13)]
    group_b = [hx, gb, gc, k_d, v_d, k_s, v_s, ki, wi]
    pad = cfg.n_b - sum(p.shape[1] for p in group_b)
    cols = [q_d, q_s, qi, z] + group_b + [jnp.zeros((w.shape[0], pad), w.dtype)]
    return jnp.concatenate(cols, axis=1).astype(BF16)


def _pad_rows(a, rows):
    return jnp.pad(a, ((0, 0), (0, rows - a.shape[1]), (0, 0)))


def kernel(x_prompt, x_sample, cache_diff_k, cache_diff_v, cache_dsa_k, cache_dsa_v, cache_dsa_kidx,
           state_conv, page_table, c_prompt, c_sample, norm_g, w_ada, b_ada, w_in, conv_w,
           diff_lam, diff_subln_g, w_out, final_g):
    nbp, t, d = x_prompt.shape
    nbs, t_new, _ = x_sample.shape
    depth, n_pool = cache_diff_k.shape[:2]
    cfg = _make_cfg(d, cache_diff_k.shape[3], cache_dsa_k.shape[3])
    dc = cfg.d_conv
    ob = cfg.off_b
    kdw, ksw = cfg.n_kv_diff * HEAD_DIM, cfg.n_kv_dsa * HEAD_DIM
    assert cache_diff_k.shape[2] == PAGE_SIZE and cache_diff_k.shape[4] == HEAD_DIM
    assert state_conv.shape[2] == CONV_WIDTH - 1 and CONV_WIDTH - 1 <= t_new <= SUBLANES
    assert cfg.n_a == 2 * d and cfg.d_diff == cfg.d_dsa

    n_c = nbp + nbs
    c_all = _pad_rows(jnp.concatenate([c_prompt, c_sample], axis=0)[None], -(-n_c // SUBLANES) * SUBLANES)[0]
    mod = _mod_call(c_all, w_ada, b_ada).reshape(depth, c_all.shape[0], 3, d)

    ck_d = cache_diff_k.reshape(depth, n_pool, PAGE_SIZE, kdw)
    cv_d = cache_diff_v.reshape(depth, n_pool, PAGE_SIZE, kdw)
    ck_s = cache_dsa_k.reshape(depth, n_pool, PAGE_SIZE, ksw)
    cv_s = cache_dsa_v.reshape(depth, n_pool, PAGE_SIZE, ksw)

    xp = x_prompt.reshape(nbp * t, d)
    xs = _pad_rows(x_sample, SUBLANES).reshape(nbs * SUBLANES, d)
    ones_halo = jnp.ones((SUBLANES, dc), F32)
    tm_p = 512
    seq_tiles = t // tm_p
    halo_blocks = tm_p // SUBLANES

    def prompt_halo(col_blk):
        return lambda i, j: (jnp.maximum(i * halo_blocks - 1, 0), col_blk)

    p_rows, s_rows = [], []
    for l in range(depth):
        lam_init = LAMBDA_INIT_BASE - LAMBDA_INIT_AMP * math.exp(-LAMBDA_INIT_RATE * l)
        wp = _pack_w_in(w_in[l], cfg)
        wo16 = w_out[l].astype(BF16)
        mod_p, mod_s = mod[l, :nbp], mod[l, nbp:n_c]

        h_p = _norm_mod_call(xp, norm_g[l], mod_p, t, 256, BF16)
        a_p = _matmul_call(h_p, wp, 0, cfg.n_a, BF16, 1024)
        b_p = _matmul_call(h_p, wp, cfg.n_a, cfg.n_b, F32, 1024)
        kd16 = b_p[:, ob["k_d"]:ob["k_d"] + kdw].astype(BF16)
        vd16 = b_p[:, ob["v_d"]:ob["v_d"] + kdw].astype(BF16)
        ks16 = b_p[:, ob["k_s"]:ob["k_s"] + ksw].astype(BF16)
        vs16 = b_p[:, ob["v_s"]:ob["v_s"] + ksw].astype(BF16)
        ki_p = b_p[:, ob["kiwi"]:ob["kiwi"] + D_IDX]
        kiki16 = jnp.concatenate([ki_p, ki_p], axis=1).astype(BF16)
        yd_p = _diff_prompt_call(a_p, kd16, vd16, diff_lam[l], diff_subln_g[l], cfg, nbp, t, lam_init)
        ys_p = _dsa_prompt_call(a_p, b_p, kiki16, ks16, vs16, cfg, nbp, t)
        xp, ut_p = _mix_out_call(xp, b_p, b_p, b_p, prompt_halo(0), prompt_halo(2), conv_w[l], yd_p, ys_p,
                                 a_p, mod_p, wo16, cfg, t, tm_p, seq_tiles)
        p_rows.append((b_p, ut_p.reshape(nbp, seq_tiles, SUBLANES, dc)[:, -1, SUBLANES - (CONV_WIDTH - 1):]))

        h_s = _norm_mod_call(xs, norm_g[l], mod_s, SUBLANES, SUBLANES, F32).astype(BF16)
        a_s = _matmul_call(h_s, wp, 0, cfg.n_a, F32, nbs * SUBLANES)
        b_s = _matmul_call(h_s, wp, cfg.n_a, cfg.n_b, F32, nbs * SUBLANES)
        b_s3 = b_s.reshape(nbs, SUBLANES, cfg.n_b)

        def new_rows(name, width):
            return _pad_rows(b_s3[:, :, ob[name]:ob[name] + width], PAGE_SIZE).astype(BF16)

        yd_s = _dec_diff_call(page_table, a_s, ck_d, cv_d, new_rows("k_d", kdw), new_rows("v_d", kdw),
                              diff_lam[l], diff_subln_g[l], cfg, l, t_new, lam_init)
        qi_off = cfg.d_diff + cfg.d_dsa
        qi_rows = a_s[:, qi_off:qi_off + N_IDX_HEADS * D_IDX].reshape(nbs, SUBLANES, N_IDX_HEADS, D_IDX)
        qi_rows = qi_rows.transpose(0, 2, 1, 3).reshape(nbs, N_IDX_HEADS * SUBLANES, D_IDX).astype(BF16)
        wi_col = b_s3[:, :, ob["kiwi"] + D_IDX:ob["kiwi"] + D_IDX + N_IDX_HEADS]
        wi_col = wi_col.transpose(0, 2, 1).reshape(nbs, N_IDX_HEADS * SUBLANES, 1)
        bias = _dec_index_call(page_table, qi_rows, wi_col, cache_dsa_kidx, new_rows("kiwi", D_IDX), l, t_new)
        ys_s = _dec_dsa_call(page_table, a_s, bias, ck_s, cv_s, new_rows("k_s", ksw), new_rows("v_s", ksw),
                             cfg, l)
        state_pad = jnp.pad(state_conv[l], ((0, 0), (SUBLANES - (CONV_WIDTH - 1), 0), (0, 0)))
        xs, ut_s = _mix_out_call(xs, b_s, state_pad.reshape(nbs * SUBLANES, dc), ones_halo,
                                 lambda i, j: (i, 0), lambda i, j: (0, 0), conv_w[l], yd_s, ys_s,
                                 a_s, mod_s, wo16, cfg, SUBLANES, SUBLANES, 0)
        s_rows.append((b_s3[:, :t_new], ut_s[:, t_new - (CONV_WIDTH - 1):t_new]))

    y_prompt = _rmsnorm_call(xp, final_g, 256).reshape(nbp, t, d)
    y_sample = _rmsnorm_call(xs, final_g, SUBLANES).reshape(nbs, SUBLANES, d)[:, :t_new]

    def stack_p(name, n_kv):
        w = n_kv * HEAD_DIM if n_kv else D_IDX
        shape = (nbp, t, n_kv, HEAD_DIM) if n_kv else (nbp, t, D_IDX)
        return jnp.stack([r[0][:, ob[name]:ob[name] + w].reshape(shape) for r in p_rows])

    def stack_s(name, n_kv):
        w = n_kv * HEAD_DIM if n_kv else D_IDX
        shape = (nbs, t_new, n_kv, HEAD_DIM) if n_kv else (nbs, t_new, D_IDX)
        return jnp.stack([r[0][:, :, ob[name]:ob[name] + w].reshape(shape) for r in s_rows])

    return (y_prompt, y_sample,
            stack_p("k_d", cfg.n_kv_diff), stack_p("v_d", cfg.n_kv_diff),
            stack_p("k_s", cfg.n_kv_dsa), stack_p("v_s", cfg.n_kv_dsa), stack_p("kiwi", 0),
            jnp.stack([r[1] for r in p_rows]),
            stack_s("k_d", cfg.n_kv_diff), stack_s("v_d", cfg.n_kv_diff),
            stack_s("k_s", cfg.n_kv_dsa), stack_s("v_s", cfg.n_kv_dsa), stack_s("kiwi", 0),
            jnp.stack([r[1] for r in s_rows]))
```

```python
import functools
import math
from typing import NamedTuple

import jax
import jax.numpy as jnp
from jax import lax
from jax.experimental import pallas as pl
from jax.experimental.pallas import tpu as pltpu

F32 = jnp.float32
BF16 = jnp.bfloat16

HEAD_DIM = 128
D_IDX = 64
N_IDX_HEADS = 16
CONV_WIDTH = 3
TOPK_MAX = 256
PAGE_SIZE = 128
RMS_EPS = 1e-6
LAMBDA_INIT_BASE = 0.8
LAMBDA_INIT_AMP = 0.6
LAMBDA_INIT_RATE = 0.3

LANES = 128
SUBLANES = 8
MIB = 2 ** 20

NEG_INF = float("-inf")
INT_MIN = -(2 ** 31)
KEY_NEG_INF = INT_MIN + 0x7FFFFF


class Cfg(NamedTuple):
    d_model: int
    d_conv: int
    n_kv_diff: int
    rep_diff: int
    d_diff: int
    n_kv_dsa: int
    rep_dsa: int
    d_dsa: int
    n_a: int
    n_b: int
    off_b: dict


def _make_cfg(d_model, n_kv_diff, n_kv_dsa):
    d_mix = d_model
    d_conv = d_mix // 4
    d_diff = 3 * d_mix // 8
    d_dsa = 3 * d_mix // 8
    n_a = d_diff + d_dsa + N_IDX_HEADS * D_IDX + d_mix
    off, pos = {}, 0
    for name, width in (("hx", d_conv), ("gb", d_conv), ("gc", d_conv),
                        ("k_d", n_kv_diff * HEAD_DIM), ("v_d", n_kv_diff * HEAD_DIM),
                        ("k_s", n_kv_dsa * HEAD_DIM), ("v_s", n_kv_dsa * HEAD_DIM),
                        ("kiwi", LANES)):
        off[name] = pos
        pos += width
    n_b = -(-pos // 512) * 512
    return Cfg(d_model, d_conv, n_kv_diff, d_diff // HEAD_DIM // n_kv_diff, d_diff,
               n_kv_dsa, d_dsa // HEAD_DIM // n_kv_dsa, d_dsa, n_a, n_b, off)


def _params(semantics, vmem_mib):
    return pltpu.CompilerParams(dimension_semantics=semantics, vmem_limit_bytes=vmem_mib * MIB)


def _dot_nt(a, b):
    return lax.dot_general(a, b, (((1,), (1,)), ((), ())), preferred_element_type=F32)


def _mod_kernel(c_ref, w_ref, b_ref, o_ref):
    @pl.when(pl.program_id(1) == 0)
    def _():
        o_ref[...] = jnp.broadcast_to(b_ref[...], o_ref.shape)

    c = c_ref[...]
    a = (c * jax.nn.sigmoid(c)).astype(BF16)
    o_ref[...] += jnp.dot(a, w_ref[...].astype(BF16), preferred_element_type=F32)


def _mod_call(c_all, w_ada, b_ada):
    depth, d, n3 = w_ada.shape
    rows = c_all.shape[0]
    tk = 256
    return pl.pallas_call(
        _mod_kernel,
        grid=(depth, d // tk),
        in_specs=[pl.BlockSpec((rows, tk), lambda l, k: (0, k)),
                  pl.BlockSpec((None, tk, n3), lambda l, k: (l, k, 0)),
                  pl.BlockSpec((None, 1, n3), lambda l, k: (l, 0, 0))],
        out_specs=pl.BlockSpec((None, rows, n3), lambda l, k: (l, 0, 0)),
        out_shape=jax.ShapeDtypeStruct((depth, rows, n3), F32),
        compiler_params=_params(("arbitrary", "arbitrary"), 48),
        name="adaln_mod",
    )(c_all, w_ada, b_ada.reshape(depth, 1, n3))


def _rms(x):
    return x * lax.rsqrt(jnp.mean(x * x, axis=-1, keepdims=True) + RMS_EPS)


def _norm_mod_kernel(x_ref, g_ref, mod_ref, o_ref):
    y = _rms(x_ref[...]) * g_ref[...]
    o_ref[...] = (y * (1.0 + mod_ref[1:2, :]) + mod_ref[0:1, :]).astype(o_ref.dtype)


def _norm_mod_call(x2d, g, mod3, rows_per_batch, tm, out_dtype):
    m, d = x2d.shape
    per = rows_per_batch // tm
    return pl.pallas_call(
        _norm_mod_kernel,
        grid=(m // tm,),
        in_specs=[pl.BlockSpec((tm, d), lambda i: (i, 0)),
                  pl.BlockSpec((1, d), lambda i: (0, 0)),
                  pl.BlockSpec((None, 3, d), lambda i: (i // per, 0, 0))],
        out_specs=pl.BlockSpec((tm, d), lambda i: (i, 0)),
        out_shape=jax.ShapeDtypeStruct((m, d), out_dtype),
        compiler_params=_params(("arbitrary",), 40),
        name="norm_mod",
    )(x2d, g.reshape(1, d), mod3)


def _rmsnorm_kernel(x_ref, g_ref, o_ref):
    o_ref[...] = _rms(x_ref[...]) * g_ref[...]


def _rmsnorm_call(x2d, g, tm):
    m, d = x2d.shape
    return pl.pallas_call(
        _rmsnorm_kernel,
        grid=(m // tm,),
        in_specs=[pl.BlockSpec((tm, d), lambda i: (i, 0)),
                  pl.BlockSpec((1, d), lambda i: (0, 0))],
        out_specs=pl.BlockSpec((tm, d), lambda i: (i, 0)),
        out_shape=jax.ShapeDtypeStruct((m, d), F32),
        compiler_params=_params(("arbitrary",), 40),
        name="final_norm",
    )(x2d, g.reshape(1, d))


def _matmul_kernel(x_ref, w_ref, o_ref):
    o_ref[...] = jnp.dot(x_ref[...], w_ref[...], preferred_element_type=F32).astype(o_ref.dtype)


def _matmul_call(x16, w16, col0, n, out_dtype, tm, tn=512):
    m, k = x16.shape
    blk0 = col0 // tn
    return pl.pallas_call(
        _matmul_kernel,
        grid=(m // tm, n // tn),
        in_specs=[pl.BlockSpec((tm, k), lambda i, j: (i, 0)),
                  pl.BlockSpec((k, tn), lambda i, j: (0, blk0 + j))],
        out_specs=pl.BlockSpec((tm, tn), lambda i, j: (i, j)),
        out_shape=jax.ShapeDtypeStruct((m, n), out_dtype),
        compiler_params=_params(("arbitrary", "arbitrary"), 48),
        name="in_proj",
    )(x16, w16)


def _half_masked(q, rows):
    lane = lax.broadcasted_iota(jnp.int32, (rows, HEAD_DIM), 1)
    zero = jnp.zeros_like(q)
    return jnp.where(lane < HEAD_DIM // 2, q, zero), jnp.where(lane >= HEAD_DIM // 2, q, zero)


def _lambda_full(lam_ref, lam_init):
    lam = lam_ref[...]
    a = jnp.sum(lam[0:1, :] * lam[1:2, :], axis=-1, keepdims=True)
    b = jnp.sum(lam[2:3, :] * lam[3:4, :], axis=-1, keepdims=True)
    return jnp.exp(a) - jnp.exp(b) + lam_init


def _diff_head_out(o0, o1, lmb, sg, lam_init):
    o = o0 - lmb * o1
    return _rms(o) * sg * (1.0 - lam_init)


def _online_update(m_sc, l_sc, acc_sc, rows, s, v16):
    m_prev = m_sc[rows]
    m_new = jnp.maximum(m_prev, jnp.max(s, axis=-1, keepdims=True))
    m_safe = jnp.where(m_new == NEG_INF, 0.0, m_new)
    alpha = jnp.exp(m_prev - m_safe)
    p = jnp.exp(s - m_safe)
    l_sc[rows] = alpha * l_sc[rows] + jnp.sum(p, axis=-1, keepdims=True)
    acc_sc[rows] = alpha * acc_sc[rows] + jnp.dot(p.astype(BF16), v16, preferred_element_type=F32)
    m_sc[rows] = m_new


def _diff_prompt_kernel(q_ref, k_ref, v_ref, lam_ref, sg_ref, o_ref, qs_sc, m_sc, l_sc, acc_sc,
                        *, rep, tq, lam_init):
    i = pl.program_id(2)
    chunk = LANES
    n_chunks = 2 * rep * tq // chunk
    for r in range(rep):
        qr = q_ref[:, r * HEAD_DIM:(r + 1) * HEAD_DIM] * ((HEAD_DIM // 2) ** -0.5)
        lo, hi = _half_masked(qr, tq)
        qs_sc[(2 * r) * tq:(2 * r + 1) * tq] = lo
        qs_sc[(2 * r + 1) * tq:(2 * r + 2) * tq] = hi
    m_sc[...] = jnp.full_like(m_sc, NEG_INF)
    l_sc[...] = jnp.zeros_like(l_sc)
    acc_sc[...] = jnp.zeros_like(acc_sc)

    def kv_block(j, diagonal):
        start = pl.multiple_of(j * tq, tq)
        k = k_ref[pl.ds(start, tq), :]
        v = v_ref[pl.ds(start, tq), :]
        for c in range(n_chunks):
            rows = slice(c * chunk, (c + 1) * chunk)
            s = _dot_nt(qs_sc[rows], k)
            if diagonal:
                q0 = (c * chunk) % tq
                row = lax.broadcasted_iota(jnp.int32, (chunk, tq), 0) + q0
                col = lax.broadcasted_iota(jnp.int32, (chunk, tq), 1)
                s = jnp.where(col <= row, s, NEG_INF)
            m_prev = m_sc[rows]
            m_new = jnp.maximum(m_prev, jnp.max(s, axis=-1, keepdims=True))
            alpha = jnp.exp(m_prev - m_new)
            p = jnp.exp(s - jnp.concatenate([m_new] * (tq // LANES), axis=1))
            l_sc[rows] = alpha * l_sc[rows] + jnp.sum(p, axis=-1, keepdims=True)
            acc_sc[rows] = alpha * acc_sc[rows] + jnp.dot(p.astype(BF16), v, preferred_element_type=F32)
            m_sc[rows] = m_new

    def full_block(j, carry):
        kv_block(j, False)
        return carry

    lax.fori_loop(0, i, full_block, 0)
    kv_block(i, True)

    lmb = _lambda_full(lam_ref, lam_init)
    for r in range(rep):
        lo = slice((2 * r) * tq, (2 * r + 1) * tq)
        hi = slice((2 * r + 1) * tq, (2 * r + 2) * tq)
        y = _diff_head_out(acc_sc[lo] / l_sc[lo], acc_sc[hi] / l_sc[hi], lmb, sg_ref[...], lam_init)
        o_ref[:, r * HEAD_DIM:(r + 1) * HEAD_DIM] = y.astype(BF16)


def _diff_prompt_call(a_p, kd16, vd16, lam, sg, cfg, nb, t, lam_init, tq=256):
    m = a_p.shape[0]
    nq = t // tq
    g_w = cfg.rep_diff * HEAD_DIM
    rows = 2 * cfg.rep_diff * tq
    kern = functools.partial(_diff_prompt_kernel, rep=cfg.rep_diff, tq=tq, lam_init=lam_init)
    return pl.pallas_call(
        kern,
        grid=(nb, cfg.n_kv_diff, nq),
        in_specs=[pl.BlockSpec((tq, g_w), lambda b, g, i: (b * nq + i, g)),
                  pl.BlockSpec((t, HEAD_DIM), lambda b, g, i: (b, g)),
                  pl.BlockSpec((t, HEAD_DIM), lambda b, g, i: (b, g)),
                  pl.BlockSpec(lam.shape, lambda b, g, i: (0, 0)),
                  pl.BlockSpec((1, HEAD_DIM), lambda b, g, i: (0, 0))],
        out_specs=pl.BlockSpec((tq, g_w), lambda b, g, i: (b * nq + i, g)),
        out_shape=jax.ShapeDtypeStruct((m, cfg.d_diff), BF16),
        scratch_shapes=[pltpu.VMEM((rows, HEAD_DIM), BF16), pltpu.VMEM((rows, LANES), F32),
                        pltpu.VMEM((rows, LANES), F32), pltpu.VMEM((rows, HEAD_DIM), F32)],
        compiler_params=_params(("arbitrary", "arbitrary", "arbitrary"), 40),
        name="diff_attn_prompt",
    )(a_p, kd16, vd16, lam, sg.reshape(1, HEAD_DIM))


def _monotone_key(x):
    bits = lax.bitcast_convert_type(x, jnp.int32)
    key = bits ^ ((bits >> 31) & jnp.int32(0x7FFFFFFF))
    return jnp.where(bits == jnp.int32(INT_MIN), 0, key)


def _count(mask):
    return jnp.sum(jnp.where(mask, 1.0, 0.0), axis=-1, keepdims=True)


def _topk_mask(score, k_top):
    r, w = score.shape
    key = _monotone_key(score)
    kf = float(k_top)

    def value_bit(it, t):
        cand = t + (jnp.int32(1) << (31 - it))
        return jnp.where(_count(key >= cand) >= kf, cand, t)

    t = lax.fori_loop(0, 32, value_bit, jnp.full((r, 1), INT_MIN, jnp.int32))
    gt = key > t
    eq = key == t
    n_gt = _count(gt)
    need = kf - n_gt
    col = lax.broadcasted_iota(jnp.int32, (r, w), 1)
    ambiguous = (n_gt + _count(eq) > kf) & (t > KEY_NEG_INF)
    n_bits = w.bit_length()

    def tie_search(_):
        def index_bit(it, j):
            cand = j + (jnp.int32(1) << (n_bits - 1 - it))
            return jnp.where(_count(eq & (col < cand)) < need, cand, j)
        return lax.fori_loop(0, n_bits, index_bit, jnp.zeros((r, 1), jnp.int32))

    any_amb = jnp.max(jnp.where(ambiguous, 1.0, 0.0)) > 0.5
    j = lax.cond(any_amb, tie_search, lambda _: jnp.full((r, 1), w, jnp.int32), 0)
    return gt | (eq & (col <= j))


def _selection_bias(score, k_top):
    sel = _topk_mask(score, k_top) & (score > NEG_INF)
    return jnp.where(sel, 0.0, NEG_INF)


def _dsa_prompt_kernel(qi_ref, kiwi_ref, qs_ref, kiki_ref, ks_ref, vs_ref, o_ref,
                       *, tq, widths, k_top, n_kv, rep, blocks_per_width):
    i = pl.program_id(1)
    for c, w in enumerate(widths):
        pl.when(i // blocks_per_width == c)(
            functools.partial(_dsa_prompt_body, qi_ref, kiwi_ref, qs_ref, kiki_ref, ks_ref, vs_ref, o_ref,
                              i=i, tq=tq, w=w, k_top=k_top, n_kv=n_kv, rep=rep))


def _dsa_prompt_body(qi_ref, kiwi_ref, qs_ref, kiki_ref, ks_ref, vs_ref, o_ref, *, i, tq, w, k_top, n_kv, rep):
    wi = kiwi_ref[...][:, D_IDX:D_IDX + N_IDX_HEADS] * (N_IDX_HEADS ** -0.5)
    kiki = kiki_ref[0:w, :]
    idx = jnp.zeros((tq, w), F32)
    for j in range(N_IDX_HEADS // 2):
        lo, hi = _half_masked(qi_ref[:, j * LANES:(j + 1) * LANES], tq)
        sc = jnp.maximum(_dot_nt(jnp.concatenate([lo, hi], axis=0), kiki), 0.0)
        idx = idx + sc[:tq] * wi[:, 2 * j:2 * j + 1] + sc[tq:] * wi[:, 2 * j + 1:2 * j + 2]
    row = lax.broadcasted_iota(jnp.int32, (tq, w), 0) + i * tq
    col = lax.broadcasted_iota(jnp.int32, (tq, w), 1)
    bias = _selection_bias(jnp.where(col <= row, idx, NEG_INF), k_top)

    scale = HEAD_DIM ** -0.5
    for g in range(n_kv):
        heads = [qs_ref[:, (g * rep + r) * HEAD_DIM:(g * rep + r + 1) * HEAD_DIM] for r in range(rep)]
        s = _dot_nt(jnp.concatenate(heads, axis=0), ks_ref[0:w, g * HEAD_DIM:(g + 1) * HEAD_DIM]) * scale
        s = s.reshape(rep, tq, w) + bias[None]
        p = jnp.exp(s - jnp.max(s, axis=-1, keepdims=True))
        l = jnp.sum(p, axis=-1, keepdims=True).reshape(rep * tq, 1)
        o = jnp.dot(p.reshape(rep * tq, w).astype(BF16), vs_ref[0:w, g * HEAD_DIM:(g + 1) * HEAD_DIM],
                    preferred_element_type=F32) / l
        for r in range(rep):
            o_ref[:, (g * rep + r) * HEAD_DIM:(g * rep + r + 1) * HEAD_DIM] = o[r * tq:(r + 1) * tq].astype(BF16)


def _dsa_prompt_call(a_p, b_p, kiki16, ks16, vs16, cfg, nb, t, tq=128):
    m = a_p.shape[0]
    nq = t // tq
    k_top = min(TOPK_MAX, t // 4)
    qi_blk = (cfg.d_diff + cfg.d_dsa) // (N_IDX_HEADS * D_IDX)
    kiwi_blk = cfg.off_b["kiwi"] // LANES
    kvw = cfg.n_kv_dsa * HEAD_DIM
    n_widths = math.gcd(nq, 4)
    widths = tuple(t * (c + 1) // n_widths for c in range(n_widths))
    kern = functools.partial(_dsa_prompt_kernel, tq=tq, widths=widths, k_top=k_top, n_kv=cfg.n_kv_dsa,
                             rep=cfg.rep_dsa, blocks_per_width=nq // n_widths)
    return pl.pallas_call(
        kern,
        grid=(nb, nq),
        in_specs=[pl.BlockSpec((tq, N_IDX_HEADS * D_IDX), lambda b, i: (b * nq + i, qi_blk)),
                  pl.BlockSpec((tq, LANES), lambda b, i: (b * nq + i, kiwi_blk)),
                  pl.BlockSpec((tq, cfg.d_dsa), lambda b, i: (b * nq + i, 1)),
                  pl.BlockSpec((t, LANES), lambda b, i: (b, 0)),
                  pl.BlockSpec((t, kvw), lambda b, i: (b, 0)),
                  pl.BlockSpec((t, kvw), lambda b, i: (b, 0))],
        out_specs=pl.BlockSpec((tq, cfg.d_dsa), lambda b, i: (b * nq + i, 0)),
        out_shape=jax.ShapeDtypeStruct((m, cfg.d_dsa), BF16),
        compiler_params=_params(("arbitrary", "arbitrary"), 56),
        name="dsa_prompt",
    )(a_p, b_p, a_p, kiki16, ks16, vs16)


def _mix_out_kernel(x_ref, hx_ref, gb_ref, gc_ref, ha_ref, hb_ref, cw_ref, yd_ref, ys_ref, z_ref,
                    mod_ref, wo_ref, o_ref, ut_ref, mix_sc, *, tm, seq_tiles, d_conv, d_diff):
    i = pl.program_id(0)

    @pl.when(pl.program_id(1) == 0)
    def _():
        u = gc_ref[...] * hx_ref[...]
        uh = ha_ref[...] * hb_ref[...]
        if seq_tiles:
            uh = jnp.where(i % seq_tiles == 0, 0.0, uh)
        row = lax.broadcasted_iota(jnp.int32, (tm, d_conv), 0)
        u1 = jnp.where(row == 0, uh[7:8], pltpu.roll(u, 1, 0))
        u2 = jnp.where(row == 0, uh[6:7], jnp.where(row == 1, uh[7:8], pltpu.roll(u, 2, 0)))
        cw = cw_ref[...]
        yc = gb_ref[...] * (cw[0:1] * u2 + cw[1:2] * u1 + cw[2:3] * u)
        ut_ref[...] = u[tm - SUBLANES:tm]

        def gated(y, c0, c1):
            zf = z_ref[:, c0:c1].astype(F32)
            mix_sc[:, c0:c1] = (y * (zf * jax.nn.sigmoid(zf))).astype(BF16)

        gated(yc, 0, d_conv)
        gated(yd_ref[...].astype(F32), d_conv, d_conv + d_diff)
        gated(ys_ref[...].astype(F32), d_conv + d_diff, mix_sc.shape[1])

    o_ref[...] = x_ref[...] + mod_ref[2:3, :] * jnp.dot(mix_sc[...], wo_ref[...], preferred_element_type=F32)


def _mix_out_call(x2d, b_x, halo_a, halo_b, halo_map_a, halo_map_b, conv_w, y_diff, y_dsa, a_x, mod3,
                  wo16, cfg, rows_per_batch, tm, seq_tiles, tn=512):
    m, d = x2d.shape
    dc = cfg.d_conv
    per = rows_per_batch // tm
    z_blk = (cfg.n_a - d) // d
    kern = functools.partial(_mix_out_kernel, tm=tm, seq_tiles=seq_tiles, d_conv=dc, d_diff=cfg.d_diff)
    return pl.pallas_call(
        kern,
        grid=(m // tm, d // tn),
        in_specs=[pl.BlockSpec((tm, tn), lambda i, j: (i, j)),
                  pl.BlockSpec((tm, dc), lambda i, j: (i, 0)),
                  pl.BlockSpec((tm, dc), lambda i, j: (i, 1)),
                  pl.BlockSpec((tm, dc), lambda i, j: (i, 2)),
                  pl.BlockSpec((SUBLANES, dc), halo_map_a),
                  pl.BlockSpec((SUBLANES, dc), halo_map_b),
                  pl.BlockSpec(conv_w.shape, lambda i, j: (0, 0)),
                  pl.BlockSpec((tm, cfg.d_diff), lambda i, j: (i, 0)),
                  pl.BlockSpec((tm, cfg.d_dsa), lambda i, j: (i, 0)),
                  pl.BlockSpec((tm, d), lambda i, j: (i, z_blk)),
                  pl.BlockSpec((None, 3, tn), lambda i, j: (i // per, 0, j)),
                  pl.BlockSpec((d, tn), lambda i, j: (0, j))],
        out_specs=[pl.BlockSpec((tm, tn), lambda i, j: (i, j)),
                   pl.BlockSpec((None, SUBLANES, dc), lambda i, j: (i, 0, 0))],
        out_shape=[jax.ShapeDtypeStruct((m, d), F32),
                   jax.ShapeDtypeStruct((m // tm, SUBLANES, dc), F32)],
        scratch_shapes=[pltpu.VMEM((tm, d), BF16)],
        compiler_params=_params(("arbitrary", "arbitrary"), 56),
        name="mix_out",
    )(x2d, b_x, b_x, b_x, halo_a, halo_b, conv_w, y_diff, y_dsa, a_x, mod3, wo16)


def _page_specs(layer, pps, rows, width):
    def spec(p):
        return pl.BlockSpec((None, None, rows, width),
                            lambda b, s, pt: (layer, pt[b, s * pps + p], 0, 0))
    return [spec(p) for p in range(pps)]


def _cat_pages(refs, g, n_kv):
    return jnp.concatenate([r[pl.ds(g, PAGE_SIZE, stride=n_kv), :] for r in refs], axis=0).astype(BF16)


def _dec_diff_kernel(pt_ref, q_ref, *refs, pps, n_kv, rep, t_new, lam_init):
    kp, vp = refs[:pps], refs[pps:2 * pps]
    knew_ref, vnew_ref, lam_ref, sg_ref, o_ref, qs_sc, m_sc, l_sc, acc_sc = refs[2 * pps:]
    step = pl.program_id(1)
    rg = 2 * rep * SUBLANES

    @pl.when(step == 0)
    def _():
        for h in range(n_kv * rep):
            qh = q_ref[:, h * HEAD_DIM:(h + 1) * HEAD_DIM] * ((HEAD_DIM // 2) ** -0.5)
            lo, hi = _half_masked(qh, SUBLANES)
            qs_sc[2 * h * SUBLANES:(2 * h + 1) * SUBLANES] = lo
            qs_sc[(2 * h + 1) * SUBLANES:(2 * h + 2) * SUBLANES] = hi
        m_sc[...] = jnp.full_like(m_sc, NEG_INF)
        l_sc[...] = jnp.zeros_like(l_sc)
        acc_sc[...] = jnp.zeros_like(acc_sc)

    for g in range(n_kv):
        rows = slice(g * rg, (g + 1) * rg)
        _online_update(m_sc, l_sc, acc_sc, rows, _dot_nt(qs_sc[rows].astype(BF16), _cat_pages(kp, g, n_kv)),
                       _cat_pages(vp, g, n_kv))

    @pl.when(step == pl.num_programs(1) - 1)
    def _():
        tok = lax.broadcasted_iota(jnp.int32, (rg, LANES), 0) % SUBLANES
        col = lax.broadcasted_iota(jnp.int32, (rg, LANES), 1)
        valid = (col <= tok) & (col < t_new)
        for g in range(n_kv):
            rows = slice(g * rg, (g + 1) * rg)
            c0, c1 = g * HEAD_DIM, (g + 1) * HEAD_DIM
            s = jnp.where(valid, _dot_nt(qs_sc[rows].astype(BF16), knew_ref[:, c0:c1]), NEG_INF)
            _online_update(m_sc, l_sc, acc_sc, rows, s, vnew_ref[:, c0:c1])
        lmb = _lambda_full(lam_ref, lam_init)
        o_all = acc_sc[...] / l_sc[...]
        for h in range(n_kv * rep):
            o0 = o_all[2 * h * SUBLANES:(2 * h + 1) * SUBLANES]
            o1 = o_all[(2 * h + 1) * SUBLANES:(2 * h + 2) * SUBLANES]
            o_ref[:, h * HEAD_DIM:(h + 1) * HEAD_DIM] = _diff_head_out(o0, o1, lmb, sg_ref[...], lam_init)


def _dec_diff_call(page_table, a_s, cache_k, cache_v, knew16, vnew16, lam, sg, cfg, layer, t_new,
                   lam_init, pps=8):
    nb, n_pages = page_table.shape
    kvw = cfg.n_kv_diff * HEAD_DIM
    rows = cfg.n_kv_diff * 2 * cfg.rep_diff * SUBLANES
    kern = functools.partial(_dec_diff_kernel, pps=pps, n_kv=cfg.n_kv_diff, rep=cfg.rep_diff,
                             t_new=t_new, lam_init=lam_init)
    grid_spec = pltpu.PrefetchScalarGridSpec(
        num_scalar_prefetch=1,
        grid=(nb, n_pages // pps),
        in_specs=[pl.BlockSpec((SUBLANES, cfg.d_diff), lambda b, s, pt: (b, 0))]
        + 2 * _page_specs(layer, pps, PAGE_SIZE * cfg.n_kv_diff, HEAD_DIM)
        + [pl.BlockSpec((None, PAGE_SIZE, kvw), lambda b, s, pt: (b, 0, 0)),
           pl.BlockSpec((None, PAGE_SIZE, kvw), lambda b, s, pt: (b, 0, 0)),
           pl.BlockSpec(lam.shape, lambda b, s, pt: (0, 0)),
           pl.BlockSpec((1, HEAD_DIM), lambda b, s, pt: (0, 0))],
        out_specs=pl.BlockSpec((SUBLANES, cfg.d_diff), lambda b, s, pt: (b, 0)),
        scratch_shapes=[pltpu.VMEM((rows, HEAD_DIM), F32), pltpu.VMEM((rows, 1), F32),
                        pltpu.VMEM((rows, 1), F32), pltpu.VMEM((rows, HEAD_DIM), F32)])
    return pl.pallas_call(
        kern,
        grid_spec=grid_spec,
        out_shape=jax.ShapeDtypeStruct((nb * SUBLANES, cfg.d_diff), F32),
        compiler_params=_params(("arbitrary", "arbitrary"), 40),
        name="diff_attn_sample",
    )(page_table, a_s, *([cache_k] * pps), *([cache_v] * pps), knew16, vnew16, lam, sg.reshape(1, HEAD_DIM))


def _dec_index_kernel(pt_ref, q_ref, w_ref, *refs, pps, k_top, t_new, past_len):
    kp = refs[:pps]
    knew_ref, bias_ref, sc_sc = refs[pps:]
    step = pl.program_id(1)
    span = pps * PAGE_SIZE

    def idx_score(s):
        s = jnp.maximum(s, 0.0) * (w_ref[...] * (N_IDX_HEADS ** -0.5))
        return jnp.sum(s.reshape(N_IDX_HEADS, SUBLANES, s.shape[-1]), axis=0)

    k_t = jnp.concatenate([r[...] for r in kp], axis=1).astype(BF16)
    off = pl.multiple_of(step * span, span)
    sc_sc[:, pl.ds(off, span)] = idx_score(jnp.dot(q_ref[...], k_t, preferred_element_type=F32))

    @pl.when(step == pl.num_programs(1) - 1)
    def _():
        tok = lax.broadcasted_iota(jnp.int32, (SUBLANES, LANES), 0)
        col = lax.broadcasted_iota(jnp.int32, (SUBLANES, LANES), 1)
        valid = (col <= tok) & (col < t_new)
        sc_sc[:, past_len:past_len + LANES] = jnp.where(valid, idx_score(_dot_nt(q_ref[...], knew_ref[...])), NEG_INF)
        bias_ref[...] = _selection_bias(sc_sc[...], k_top)


def _dec_index_call(page_table, qi_rows, wi_col, cache_kidx, kinew16, layer, t_new, pps=16):
    nb, n_pages = page_table.shape
    past_len = n_pages * PAGE_SIZE
    k_top = min(TOPK_MAX, (past_len + t_new) // 4)
    w = past_len + LANES
    rows = N_IDX_HEADS * SUBLANES
    kern = functools.partial(_dec_index_kernel, pps=pps, k_top=k_top, t_new=t_new, past_len=past_len)
    grid_spec = pltpu.PrefetchScalarGridSpec(
        num_scalar_prefetch=1,
        grid=(nb, n_pages // pps),
        in_specs=[pl.BlockSpec((None, rows, D_IDX), lambda b, s, pt: (b, 0, 0)),
                  pl.BlockSpec((None, rows, 1), lambda b, s, pt: (b, 0, 0))]
        + _page_specs(layer, pps, D_IDX, PAGE_SIZE)
        + [pl.BlockSpec((None, PAGE_SIZE, D_IDX), lambda b, s, pt: (b, 0, 0))],
        out_specs=pl.BlockSpec((None, SUBLANES, w), lambda b, s, pt: (b, 0, 0)),
        scratch_shapes=[pltpu.VMEM((SUBLANES, w), F32)])
    return pl.pallas_call(
        kern,
        grid_spec=grid_spec,
        out_shape=jax.ShapeDtypeStruct((nb, SUBLANES, w), F32),
        compiler_params=_params(("arbitrary", "arbitrary"), 40),
        name="dsa_index_sample",
    )(page_table, qi_rows, wi_col, *([cache_kidx] * pps), kinew16)


def _dec_dsa_kernel(pt_ref, q_ref, bias_ref, bnew_ref, *refs, pps, n_kv, rep):
    kp, vp = refs[:pps], refs[pps:2 * pps]
    knew_ref, vnew_ref, o_ref, qs_sc, m_sc, l_sc, acc_sc = refs[2 * pps:]
    step = pl.program_id(1)
    rg = rep * SUBLANES
    scale = HEAD_DIM ** -0.5

    @pl.when(step == 0)
    def _():
        for h in range(n_kv * rep):
            qs_sc[h * SUBLANES:(h + 1) * SUBLANES] = q_ref[:, h * HEAD_DIM:(h + 1) * HEAD_DIM]
        m_sc[...] = jnp.full_like(m_sc, NEG_INF)
        l_sc[...] = jnp.zeros_like(l_sc)
        acc_sc[...] = jnp.zeros_like(acc_sc)

    def attend(g, k16, v16, bias):
        rows = slice(g * rg, (g + 1) * rg)
        s = _dot_nt(qs_sc[rows].astype(BF16), k16) * scale
        n = s.shape[-1]
        s = (s.reshape(rep, SUBLANES, n) + bias[None]).reshape(rg, n)
        _online_update(m_sc, l_sc, acc_sc, rows, s, v16)

    for g in range(n_kv):
        attend(g, _cat_pages(kp, g, n_kv), _cat_pages(vp, g, n_kv), bias_ref[...])

    @pl.when(step == pl.num_programs(1) - 1)
    def _():
        for g in range(n_kv):
            c0, c1 = g * HEAD_DIM, (g + 1) * HEAD_DIM
            attend(g, knew_ref[:, c0:c1], vnew_ref[:, c0:c1], bnew_ref[...])
        o_all = acc_sc[...] / l_sc[...]
        for h in range(n_kv * rep):
            o_ref[:, h * HEAD_DIM:(h + 1) * HEAD_DIM] = o_all[h * SUBLANES:(h + 1) * SUBLANES]


def _dec_dsa_call(page_table, a_s, bias, cache_k, cache_v, knew16, vnew16, cfg, layer, pps=8):
    nb, n_pages = page_table.shape
    kvw = cfg.n_kv_dsa * HEAD_DIM
    rows = cfg.n_kv_dsa * cfg.rep_dsa * SUBLANES
    span = pps * PAGE_SIZE
    new_blk = n_pages * PAGE_SIZE // LANES
    kern = functools.partial(_dec_dsa_kernel, pps=pps, n_kv=cfg.n_kv_dsa, rep=cfg.rep_dsa)
    grid_spec = pltpu.PrefetchScalarGridSpec(
        num_scalar_prefetch=1,
        grid=(nb, n_pages // pps),
        in_specs=[pl.BlockSpec((SUBLANES, cfg.d_dsa), lambda b, s, pt: (b, 1)),
                  pl.BlockSpec((None, SUBLANES, span), lambda b, s, pt: (b, 0, s)),
                  pl.BlockSpec((None, SUBLANES, LANES), lambda b, s, pt: (b, 0, new_blk))]
        + 2 * _page_specs(layer, pps, PAGE_SIZE * cfg.n_kv_dsa, HEAD_DIM)
        + [pl.BlockSpec((None, PAGE_SIZE, kvw), lambda b, s, pt: (b, 0, 0)),
           pl.BlockSpec((None, PAGE_SIZE, kvw), lambda b, s, pt: (b, 0, 0))],
        out_specs=pl.BlockSpec((SUBLANES, cfg.d_dsa), lambda b, s, pt: (b, 0)),
        scratch_shapes=[pltpu.VMEM((rows, HEAD_DIM), F32), pltpu.VMEM((rows, 1), F32),
                        pltpu.VMEM((rows, 1), F32), pltpu.VMEM((rows, HEAD_DIM), F32)])
    return pl.pallas_call(
        kern,
        grid_spec=grid_spec,
        out_shape=jax.ShapeDtypeStruct((nb * SUBLANES, cfg.d_dsa), F32),
        compiler_params=_params(("arbitrary", "arbitrary"), 40),
        name="dsa_attn_sample",
    )(page_table, a_s, bias, bias, *([cache_k] * pps), *([cache_v] * pps), knew16, vnew16)


def _pack_w_in(w, cfg):
    dc, dd, ds = cfg.d_conv, cfg.d_diff, cfg.d_dsa
    kd, ks = cfg.n_kv_diff * HEAD_DIM, cfg.n_kv_dsa * HEAD_DIM
    widths = (dc, dc, dc, dd, kd, kd, ds, ks, ks, N_IDX_HEADS * D_IDX, D_IDX, N_IDX_HEADS, cfg.d_model)
    cuts = [0]
    for wd in widths:
        cuts.append(cuts[-1] + wd)
    assert cuts[-1] == w.shape[1]
    hx, gb, gc, q_d, k_d, v_d, q_s, k_s, v_s, qi, ki, wi, z = [w[:, cuts[n]:cuts[n + 1]] for n in range(13)]
    group_b = [hx, gb, gc, k_d, v_d, k_s, v_s, ki, wi]
    pad = cfg.n_b - sum(p.shape[1] for p in group_b)
    cols = [q_d, q_s, qi, z] + group_b + [jnp.zeros((w.shape[0], pad), w.dtype)]
    return jnp.concatenate(cols, axis=1).astype(BF16)


def _pad_rows(a, rows):
    return jnp.pad(a, ((0, 0), (0, rows - a.shape[1]), (0, 0)))


def kernel(x_prompt, x_sample, cache_diff_k, cache_diff_v, cache_dsa_k, cache_dsa_v, cache_dsa_kidx,
           state_conv, page_table, c_prompt, c_sample, norm_g, w_ada, b_ada, w_in, conv_w,
           diff_lam, diff_subln_g, w_out, final_g):
    nbp, t, d = x_prompt.shape
    nbs, t_new, _ = x_sample.shape
    depth, n_pool = cache_diff_k.shape[:2]
    cfg = _make_cfg(d, cache_diff_k.shape[3], cache_dsa_k.shape[3])
    dc = cfg.d_conv
    ob = cfg.off_b
    kdw, ksw = cfg.n_kv_diff * HEAD_DIM, cfg.n_kv_dsa * HEAD_DIM
    assert cache_diff_k.shape[2] == PAGE_SIZE and cache_diff_k.shape[4] == HEAD_DIM
    assert state_conv.shape[2] == CONV_WIDTH - 1 and CONV_WIDTH - 1 <= t_new <= SUBLANES
    assert cfg.n_a == 2 * d and cfg.d_diff == cfg.d_dsa

    n_c = nbp + nbs
    c_all = _pad_rows(jnp.concatenate([c_prompt, c_sample], axis=0)[None], -(-n_c // SUBLANES) * SUBLANES)[0]
    mod = _mod_call(c_all, w_ada, b_ada).reshape(depth, c_all.shape[0], 3, d)

    ck_d = cache_diff_k.reshape(depth, n_pool, PAGE_SIZE * cfg.n_kv_diff, HEAD_DIM)
    cv_d = cache_diff_v.reshape(depth, n_pool, PAGE_SIZE * cfg.n_kv_diff, HEAD_DIM)
    ck_s = cache_dsa_k.reshape(depth, n_pool, PAGE_SIZE * cfg.n_kv_dsa, HEAD_DIM)
    cv_s = cache_dsa_v.reshape(depth, n_pool, PAGE_SIZE * cfg.n_kv_dsa, HEAD_DIM)
    ck_i = jnp.swapaxes(cache_dsa_kidx, 2, 3)

    xp = x_prompt.reshape(nbp * t, d)
    xs = _pad_rows(x_sample, SUBLANES).reshape(nbs * SUBLANES, d)
    ones_halo = jnp.ones((SUBLANES, dc), F32)
    tm_p = 512
    seq_tiles = t // tm_p
    halo_blocks = tm_p // SUBLANES

    def prompt_halo(col_blk):
        return lambda i, j: (jnp.maximum(i * halo_blocks - 1, 0), col_blk)

    p_rows, s_rows = [], []
    for l in range(depth):
        lam_init = LAMBDA_INIT_BASE - LAMBDA_INIT_AMP * math.exp(-LAMBDA_INIT_RATE * l)
        wp = _pack_w_in(w_in[l], cfg)
        wo16 = w_out[l].astype(BF16)
        mod_p, mod_s = mod[l, :nbp], mod[l, nbp:n_c]

        h_p = _norm_mod_call(xp, norm_g[l], mod_p, t, 256, BF16)
        a_p = _matmul_call(h_p, wp, 0, cfg.n_a, BF16, 1024)
        b_p = _matmul_call(h_p, wp, cfg.n_a, cfg.n_b, F32, 1024)
        kd16 = b_p[:, ob["k_d"]:ob["k_d"] + kdw].astype(BF16)
        vd16 = b_p[:, ob["v_d"]:ob["v_d"] + kdw].astype(BF16)
        ks16 = b_p[:, ob["k_s"]:ob["k_s"] + ksw].astype(BF16)
        vs16 = b_p[:, ob["v_s"]:ob["v_s"] + ksw].astype(BF16)
        ki_p = b_p[:, ob["kiwi"]:ob["kiwi"] + D_IDX]
        kiki16 = jnp.concatenate([ki_p, ki_p], axis=1).astype(BF16)
        yd_p = _diff_prompt_call(a_p, kd16, vd16, diff_lam[l], diff_subln_g[l], cfg, nbp, t, lam_init)
        ys_p = _dsa_prompt_call(a_p, b_p, kiki16, ks16, vs16, cfg, nbp, t)
        xp, ut_p = _mix_out_call(xp, b_p, b_p, b_p, prompt_halo(0), prompt_halo(2), conv_w[l], yd_p, ys_p,
                                 a_p, mod_p, wo16, cfg, t, tm_p, seq_tiles)
        p_rows.append((b_p, ut_p.reshape(nbp, seq_tiles, SUBLANES, dc)[:, -1, SUBLANES - (CONV_WIDTH - 1):]))

        h_s = _norm_mod_call(xs, norm_g[l], mod_s, SUBLANES, SUBLANES, F32).astype(BF16)
        a_s = _matmul_call(h_s, wp, 0, cfg.n_a, F32, nbs * SUBLANES)
        b_s = _matmul_call(h_s, wp, cfg.n_a, cfg.n_b, F32, nbs * SUBLANES)
        b_s3 = b_s.reshape(nbs, SUBLANES, cfg.n_b)

        def new_rows(name, width):
            return _pad_rows(b_s3[:, :, ob[name]:ob[name] + width], PAGE_SIZE).astype(BF16)

        yd_s = _dec_diff_call(page_table, a_s, ck_d, cv_d, new_rows("k_d", kdw), new_rows("v_d", kdw),
                              diff_lam[l], diff_subln_g[l], cfg, l, t_new, lam_init)
        qi_off = cfg.d_diff + cfg.d_dsa
        qi_rows = a_s[:, qi_off:qi_off + N_IDX_HEADS * D_IDX].reshape(nbs, SUBLANES, N_IDX_HEADS, D_IDX)
        qi_rows = qi_rows.transpose(0, 2, 1, 3).reshape(nbs, N_IDX_HEADS * SUBLANES, D_IDX).astype(BF16)
        wi_col = b_s3[:, :, ob["kiwi"] + D_IDX:ob["kiwi"] + D_IDX + N_IDX_HEADS]
        wi_col = wi_col.transpose(0, 2, 1).reshape(nbs, N_IDX_HEADS * SUBLANES, 1)
        bias = _dec_index_call(page_table, qi_rows, wi_col, ck_i, new_rows("kiwi", D_IDX), l, t_new)
        ys_s = _dec_dsa_call(page_table, a_s, bias, ck_s, cv_s, new_rows("k_s", ksw), new_rows("v_s", ksw),
                             cfg, l)
        state_pad = jnp.pad(state_conv[l], ((0, 0), (SUBLANES - (CONV_WIDTH - 1), 0), (0, 0)))
        xs, ut_s = _mix_out_call(xs, b_s, state_pad.reshape(nbs * SUBLANES, dc), ones_halo,
                                 lambda i, j: (i, 0), lambda i, j: (0, 0), conv_w[l], yd_s, ys_s,
                                 a_s, mod_s, wo16, cfg, SUBLANES, SUBLANES, 0)
        s_rows.append((b_s3[:, :t_new], ut_s[:, t_new - (CONV_WIDTH - 1):t_new]))

    y_prompt = _rmsnorm_call(xp, final_g, 256).reshape(nbp, t, d)
    y_sample = _rmsnorm_call(xs, final_g, SUBLANES).reshape(nbs, SUBLANES, d)[:, :t_new]

    def stack_p(name, n_kv):
        w = n_kv * HEAD_DIM if n_kv else D_IDX
        shape = (nbp, t, n_kv, HEAD_DIM) if n_kv else (nbp, t, D_IDX)
        return jnp.stack([r[0][:, ob[name]:ob[name] + w].reshape(shape) for r in p_rows])

    def stack_s(name, n_kv):
        w = n_kv * HEAD_DIM if n_kv else D_IDX
        shape = (nbs, t_new, n_kv, HEAD_DIM) if n_kv else (nbs, t_new, D_IDX)
        return jnp.stack([r[0][:, :, ob[name]:ob[name] + w].reshape(shape) for r in s_rows])

    return (y_prompt, y_sample,
            stack_p("k_d", cfg.n_kv_diff), stack_p("v_d", cfg.n_kv_diff),
            stack_p("k_s", cfg.n_kv_dsa), stack_p("v_s", cfg.n_kv_dsa), stack_p("kiwi", 0),
            jnp.stack([r[1] for r in p_rows]),
            stack_s("k_d", cfg.n_kv_diff), stack_s("v_d", cfg.n_kv_diff),
            stack_s("k_s", cfg.n_kv_dsa), stack_s("v_s", cfg.n_kv_dsa), stack_s("kiwi", 0),
            jnp.stack([r[1] for r in s_rows]))
```

```python
import functools
import math
from typing import NamedTuple

import jax
import jax.numpy as jnp
from jax import lax
from jax.experimental import pallas as pl
from jax.experimental.pallas import tpu as pltpu

F32 = jnp.float32
BF16 = jnp.bfloat16

HEAD_DIM = 128
D_IDX = 64
N_IDX_HEADS = 16
CONV_WIDTH = 3
TOPK_MAX = 256
PAGE_SIZE = 128
RMS_EPS = 1e-6
LAMBDA_INIT_BASE = 0.8
LAMBDA_INIT_AMP = 0.6
LAMBDA_INIT_RATE = 0.3

LANES = 128
SUBLANES = 8
MIB = 2 ** 20
PROJ_TN = 512

NEG_INF = float("-inf")
INT_MIN = -(2 ** 31)
KEY_NEG_INF = INT_MIN + 0x7FFFFF


class Cfg(NamedTuple):
    d_model: int
    d_conv: int
    n_kv_diff: int
    rep_diff: int
    d_diff: int
    n_kv_dsa: int
    rep_dsa: int
    d_dsa: int
    n_main: int
    off: dict
    perm: tuple


def _make_cfg(d_model, n_kv_diff, n_kv_dsa):
    d_conv = d_model // 4
    d_diff = 3 * d_model // 8
    d_dsa = 3 * d_model // 8
    kd, ks = n_kv_diff * HEAD_DIM, n_kv_dsa * HEAD_DIM
    qi = N_IDX_HEADS * D_IDX
    w_order = (("hx", d_conv), ("gb", d_conv), ("gc", d_conv), ("q_d", d_diff), ("k_d", kd), ("v_d", kd),
               ("q_s", d_dsa), ("kv_s", 2 * ks), ("qi", qi))
    out_order = ("q_d", "q_s", "qi", "hx", "gb", "gc", "k_d", "v_d", "kv_s")
    width = dict(w_order)
    off, pos = {}, 0
    for name in out_order:
        assert width[name] % PROJ_TN == 0
        off[name] = pos
        pos += width[name]
    off["k_s"], off["v_s"] = off["kv_s"], off["kv_s"] + ks
    perm = []
    for name, wd in w_order:
        perm.extend(off[name] // PROJ_TN + b for b in range(wd // PROJ_TN))
    return Cfg(d_model, d_conv, n_kv_diff, d_diff // HEAD_DIM // n_kv_diff, d_diff,
               n_kv_dsa, d_dsa // HEAD_DIM // n_kv_dsa, d_dsa, pos, off, tuple(perm))


def _params(semantics, vmem_mib):
    return pltpu.CompilerParams(dimension_semantics=semantics, vmem_limit_bytes=vmem_mib * MIB)


def _dot_nt(a, b):
    return lax.dot_general(a, b, (((1,), (1,)), ((), ())), preferred_element_type=F32)


MOD_STREAMS = 8
MOD_ROWS = 32


def _mod_kernel(c_ref, *refs):
    w_refs, (b_ref, o_ref) = refs[:MOD_STREAMS], refs[MOD_STREAMS:]

    @pl.when(pl.program_id(1) == 0)
    def _():
        o_ref[...] = jnp.broadcast_to(b_ref[...], o_ref.shape)

    c = c_ref[...]
    a = (c * jax.nn.sigmoid(c)).astype(BF16)
    w = jnp.concatenate([r[...].astype(BF16) for r in w_refs], axis=0)
    o_ref[...] += jnp.dot(a, w, preferred_element_type=F32)


def _mod_call(c_all, w_ada, b_ada):
    depth, d, n3 = w_ada.shape
    rows = c_all.shape[0]
    tk = MOD_STREAMS * MOD_ROWS

    def w_spec(p):
        return pl.BlockSpec((None, MOD_ROWS, n3), lambda l, k: (l, k * MOD_STREAMS + p, 0))

    return pl.pallas_call(
        _mod_kernel,
        grid=(depth, d // tk),
        in_specs=[pl.BlockSpec((rows, tk), lambda l, k: (0, k))]
        + [w_spec(p) for p in range(MOD_STREAMS)]
        + [pl.BlockSpec((None, 1, n3), lambda l, k: (l, 0, 0))],
        out_specs=pl.BlockSpec((None, rows, n3), lambda l, k: (l, 0, 0)),
        out_shape=jax.ShapeDtypeStruct((depth, rows, n3), F32),
        compiler_params=_params(("arbitrary", "arbitrary"), 48),
        name="adaln_mod",
    )(c_all, *([w_ada] * MOD_STREAMS), b_ada.reshape(depth, 1, n3))


def _rms(x):
    return x * lax.rsqrt(jnp.mean(x * x, axis=-1, keepdims=True) + RMS_EPS)


def _norm_mod_kernel(x_ref, g_ref, mod_ref, o_ref):
    y = _rms(x_ref[...]) * g_ref[...]
    o_ref[...] = (y * (1.0 + mod_ref[1:2, :]) + mod_ref[0:1, :]).astype(o_ref.dtype)


def _norm_mod_call(x2d, g, mod3, rows_per_batch, tm, out_dtype):
    m, d = x2d.shape
    per = rows_per_batch // tm
    return pl.pallas_call(
        _norm_mod_kernel,
        grid=(m // tm,),
        in_specs=[pl.BlockSpec((tm, d), lambda i: (i, 0)),
                  pl.BlockSpec((1, d), lambda i: (0, 0)),
                  pl.BlockSpec((None, 3, d), lambda i: (i // per, 0, 0))],
        out_specs=pl.BlockSpec((tm, d), lambda i: (i, 0)),
        out_shape=jax.ShapeDtypeStruct((m, d), out_dtype),
        compiler_params=_params(("arbitrary",), 40),
        name="norm_mod",
    )(x2d, g.reshape(1, d), mod3)


def _rmsnorm_kernel(x_ref, g_ref, o_ref):
    o_ref[...] = _rms(x_ref[...]) * g_ref[...]


def _rmsnorm_call(x2d, g, tm):
    m, d = x2d.shape
    return pl.pallas_call(
        _rmsnorm_kernel,
        grid=(m // tm,),
        in_specs=[pl.BlockSpec((tm, d), lambda i: (i, 0)),
                  pl.BlockSpec((1, d), lambda i: (0, 0))],
        out_specs=pl.BlockSpec((tm, d), lambda i: (i, 0)),
        out_shape=jax.ShapeDtypeStruct((m, d), F32),
        compiler_params=_params(("arbitrary",), 40),
        name="final_norm",
    )(x2d, g.reshape(1, d))


def _matmul_kernel(perm_ref, x_ref, w_ref, o_ref):
    o_ref[...] = jnp.dot(x_ref[...], w_ref[...], preferred_element_type=F32).astype(o_ref.dtype)


def _matmul_call(x16, w16, layer, perm, out_dtype, tm, tn):
    m, k = x16.shape
    n_blocks = len(perm)
    grid_spec = pltpu.PrefetchScalarGridSpec(
        num_scalar_prefetch=1,
        grid=(m // tm, n_blocks),
        in_specs=[pl.BlockSpec((tm, k), lambda i, j, pm: (i, 0)),
                  pl.BlockSpec((None, k, tn), lambda i, j, pm: (layer, 0, j))],
        out_specs=pl.BlockSpec((tm, tn), lambda i, j, pm: (i, pm[j])))
    return pl.pallas_call(
        _matmul_kernel,
        grid_spec=grid_spec,
        out_shape=jax.ShapeDtypeStruct((m, n_blocks * tn), out_dtype),
        compiler_params=_params(("arbitrary", "arbitrary"), 48),
        name="in_proj",
    )(jnp.asarray(perm, jnp.int32), x16, w16)


def _half_masked(q, rows):
    lane = lax.broadcasted_iota(jnp.int32, (rows, HEAD_DIM), 1)
    zero = jnp.zeros_like(q)
    return jnp.where(lane < HEAD_DIM // 2, q, zero), jnp.where(lane >= HEAD_DIM // 2, q, zero)


def _lambda_full(lam_ref, lam_init):
    lam = lam_ref[...]
    a = jnp.sum(lam[0:1, :] * lam[1:2, :], axis=-1, keepdims=True)
    b = jnp.sum(lam[2:3, :] * lam[3:4, :], axis=-1, keepdims=True)
    return jnp.exp(a) - jnp.exp(b) + lam_init


def _diff_head_out(o0, o1, lmb, sg, lam_init):
    o = o0 - lmb * o1
    return _rms(o) * sg * (1.0 - lam_init)


def _online_update(m_sc, l_sc, acc_sc, rows, s, v16, guard_empty):
    m_prev = m_sc[rows]
    m_new = jnp.maximum(m_prev, jnp.max(s, axis=-1, keepdims=True))
    m_use = jnp.where(m_new == NEG_INF, 0.0, m_new) if guard_empty else m_new
    alpha = jnp.exp(m_prev - m_use)
    p = jnp.exp(s - jnp.concatenate([m_use] * (s.shape[-1] // LANES), axis=1))
    l_sc[rows] = alpha * l_sc[rows] + jnp.sum(p, axis=-1, keepdims=True)
    acc_sc[rows] = alpha * acc_sc[rows] + jnp.dot(p.astype(BF16), v16, preferred_element_type=F32)
    m_sc[rows] = m_new


def _init_flash(m_sc, l_sc, acc_sc):
    m_sc[...] = jnp.full_like(m_sc, NEG_INF)
    l_sc[...] = jnp.zeros_like(l_sc)
    acc_sc[...] = jnp.zeros_like(acc_sc)


def _diff_prompt_kernel(q_ref, k_ref, v_ref, lam_ref, sg_ref, o_ref, qs_sc, m_sc, l_sc, acc_sc,
                        *, rep, tq, lam_init):
    i = pl.program_id(2)
    chunk = LANES
    n_chunks = 2 * rep * tq // chunk
    for r in range(rep):
        qr = q_ref[:, r * HEAD_DIM:(r + 1) * HEAD_DIM] * ((HEAD_DIM // 2) ** -0.5)
        lo, hi = _half_masked(qr, tq)
        qs_sc[(2 * r) * tq:(2 * r + 1) * tq] = lo.astype(BF16)
        qs_sc[(2 * r + 1) * tq:(2 * r + 2) * tq] = hi.astype(BF16)
    _init_flash(m_sc, l_sc, acc_sc)

    def kv_block(j, diagonal):
        start = pl.multiple_of(j * tq, tq)
        k = k_ref[pl.ds(start, tq), :].astype(BF16)
        v = v_ref[pl.ds(start, tq), :].astype(BF16)
        for c in range(n_chunks):
            rows = slice(c * chunk, (c + 1) * chunk)
            s = _dot_nt(qs_sc[rows], k)
            if diagonal:
                q0 = (c * chunk) % tq
                row = lax.broadcasted_iota(jnp.int32, (chunk, tq), 0) + q0
                col = lax.broadcasted_iota(jnp.int32, (chunk, tq), 1)
                s = jnp.where(col <= row, s, NEG_INF)
            _online_update(m_sc, l_sc, acc_sc, rows, s, v, False)

    def full_block(j, carry):
        kv_block(j, False)
        return carry

    lax.fori_loop(0, i, full_block, 0)
    kv_block(i, True)

    lmb = _lambda_full(lam_ref, lam_init)
    for r in range(rep):
        lo = slice((2 * r) * tq, (2 * r + 1) * tq)
        hi = slice((2 * r + 1) * tq, (2 * r + 2) * tq)
        y = _diff_head_out(acc_sc[lo] / l_sc[lo], acc_sc[hi] / l_sc[hi], lmb, sg_ref[...], lam_init)
        o_ref[:, r * HEAD_DIM:(r + 1) * HEAD_DIM] = y.astype(BF16)


def _diff_prompt_call(p_p, lam, sg, cfg, nb, t, lam_init, tq=256):
    m = p_p.shape[0]
    nq = t // tq
    g_w = cfg.rep_diff * HEAD_DIM
    rows = 2 * cfg.rep_diff * tq
    q_blk, k_blk, v_blk = cfg.off["q_d"] // g_w, cfg.off["k_d"] // HEAD_DIM, cfg.off["v_d"] // HEAD_DIM
    kern = functools.partial(_diff_prompt_kernel, rep=cfg.rep_diff, tq=tq, lam_init=lam_init)
    return pl.pallas_call(
        kern,
        grid=(nb, cfg.n_kv_diff, nq),
        in_specs=[pl.BlockSpec((tq, g_w), lambda b, g, i: (b * nq + i, q_blk + g)),
                  pl.BlockSpec((t, HEAD_DIM), lambda b, g, i: (b, k_blk + g)),
                  pl.BlockSpec((t, HEAD_DIM), lambda b, g, i: (b, v_blk + g)),
                  pl.BlockSpec(lam.shape, lambda b, g, i: (0, 0)),
                  pl.BlockSpec((1, HEAD_DIM), lambda b, g, i: (0, 0))],
        out_specs=pl.BlockSpec((tq, g_w), lambda b, g, i: (b * nq + i, g)),
        out_shape=jax.ShapeDtypeStruct((m, cfg.d_diff), BF16),
        scratch_shapes=[pltpu.VMEM((rows, HEAD_DIM), BF16), pltpu.VMEM((rows, LANES), F32),
                        pltpu.VMEM((rows, LANES), F32), pltpu.VMEM((rows, HEAD_DIM), F32)],
        compiler_params=_params(("arbitrary", "arbitrary", "arbitrary"), 40),
        name="diff_attn_prompt",
    )(p_p, p_p, p_p, lam, sg.reshape(1, HEAD_DIM))


def _monotone_key(x):
    bits = lax.bitcast_convert_type(x, jnp.int32)
    key = bits ^ ((bits >> 31) & jnp.int32(0x7FFFFFFF))
    return jnp.where(bits == jnp.int32(INT_MIN), 0, key)


def _count(mask):
    return jnp.sum(jnp.where(mask, 1.0, 0.0), axis=-1, keepdims=True)


def _topk_mask(score, k_top):
    r, w = score.shape
    key = _monotone_key(score)
    kf = float(k_top)

    def value_bit(it, t):
        cand = t + (jnp.int32(1) << (31 - it))
        return jnp.where(_count(key >= cand) >= kf, cand, t)

    t = lax.fori_loop(0, 32, value_bit, jnp.full((r, 1), INT_MIN, jnp.int32))
    gt = key > t
    eq = key == t
    n_gt = _count(gt)
    need = kf - n_gt
    col = lax.broadcasted_iota(jnp.int32, (r, w), 1)
    ambiguous = (n_gt + _count(eq) > kf) & (t > KEY_NEG_INF)
    n_bits = w.bit_length()

    def tie_search(_):
        def index_bit(it, j):
            cand = j + (jnp.int32(1) << (n_bits - 1 - it))
            return jnp.where(_count(eq & (col < cand)) < need, cand, j)
        return lax.fori_loop(0, n_bits, index_bit, jnp.zeros((r, 1), jnp.int32))

    any_amb = jnp.max(jnp.where(ambiguous, 1.0, 0.0)) > 0.5
    j = lax.cond(any_amb, tie_search, lambda _: jnp.full((r, 1), w, jnp.int32), 0)
    return gt | (eq & (col <= j))


def _selection_bias(score, k_top):
    sel = _topk_mask(score, k_top) & (score > NEG_INF)
    return jnp.where(sel, 0.0, NEG_INF)


def _dsa_prompt_kernel(qi_ref, wi_ref, qs_ref, ki_ref, ks_ref, vs_ref, o_ref, kiki_sc, ks_sc, vs_sc,
                       *, tq, widths, k_top, n_kv, rep, blocks_per_width):
    i = pl.program_id(1)

    @pl.when(i == 0)
    def _():
        ki = ki_ref[...]
        lane = lax.broadcasted_iota(jnp.int32, ki.shape, 1)
        kiki_sc[...] = jnp.where(lane < D_IDX, ki, pltpu.roll(ki, D_IDX, 1)).astype(BF16)
        ks_sc[...] = ks_ref[...].astype(BF16)
        vs_sc[...] = vs_ref[...].astype(BF16)

    for c, w in enumerate(widths):
        pl.when(i // blocks_per_width == c)(
            functools.partial(_dsa_prompt_body, qi_ref, wi_ref, qs_ref, kiki_sc, ks_sc, vs_sc, o_ref,
                              i=i, tq=tq, w=w, k_top=k_top, n_kv=n_kv, rep=rep))


def _dsa_prompt_body(qi_ref, wi_ref, qs_ref, kiki_sc, ks_sc, vs_sc, o_ref, *, i, tq, w, k_top, n_kv, rep):
    wi = wi_ref[...][:, D_IDX:D_IDX + N_IDX_HEADS] * (N_IDX_HEADS ** -0.5)
    kiki = kiki_sc[0:w, :]
    idx = jnp.zeros((tq, w), F32)
    for j in range(N_IDX_HEADS // 2):
        lo, hi = _half_masked(qi_ref[:, j * LANES:(j + 1) * LANES], tq)
        sc = jnp.maximum(_dot_nt(jnp.concatenate([lo, hi], axis=0).astype(BF16), kiki), 0.0)
        idx = idx + sc[:tq] * wi[:, 2 * j:2 * j + 1] + sc[tq:] * wi[:, 2 * j + 1:2 * j + 2]
    row = lax.broadcasted_iota(jnp.int32, (tq, w), 0) + i * tq
    col = lax.broadcasted_iota(jnp.int32, (tq, w), 1)
    bias = _selection_bias(jnp.where(col <= row, idx, NEG_INF), k_top)

    scale = HEAD_DIM ** -0.5
    for g in range(n_kv):
        heads = [qs_ref[:, (g * rep + r) * HEAD_DIM:(g * rep + r + 1) * HEAD_DIM] for r in range(rep)]
        q = jnp.concatenate(heads, axis=0).astype(BF16)
        s = _dot_nt(q, ks_sc[0:w, g * HEAD_DIM:(g + 1) * HEAD_DIM]) * scale
        s = s.reshape(rep, tq, w) + bias[None]
        p = jnp.exp(s - jnp.max(s, axis=-1, keepdims=True))
        l = jnp.sum(p, axis=-1, keepdims=True).reshape(rep * tq, 1)
        o = jnp.dot(p.reshape(rep * tq, w).astype(BF16), vs_sc[0:w, g * HEAD_DIM:(g + 1) * HEAD_DIM],
                    preferred_element_type=F32) / l
        for r in range(rep):
            o_ref[:, (g * rep + r) * HEAD_DIM:(g * rep + r + 1) * HEAD_DIM] = o[r * tq:(r + 1) * tq].astype(BF16)


def _dsa_prompt_call(p_p, kiwi_p, cfg, nb, t, tq=128):
    m = p_p.shape[0]
    nq = t // tq
    k_top = min(TOPK_MAX, t // 4)
    qi_w = N_IDX_HEADS * D_IDX
    kvw = cfg.n_kv_dsa * HEAD_DIM
    qi_blk, qs_blk = cfg.off["qi"] // qi_w, cfg.off["q_s"] // cfg.d_dsa
    ks_blk, vs_blk = cfg.off["k_s"] // kvw, cfg.off["v_s"] // kvw
    n_widths = math.gcd(nq, 4)
    widths = tuple(t * (c + 1) // n_widths for c in range(n_widths))
    kern = functools.partial(_dsa_prompt_kernel, tq=tq, widths=widths, k_top=k_top, n_kv=cfg.n_kv_dsa,
                             rep=cfg.rep_dsa, blocks_per_width=nq // n_widths)
    return pl.pallas_call(
        kern,
        grid=(nb, nq),
        in_specs=[pl.BlockSpec((tq, qi_w), lambda b, i: (b * nq + i, qi_blk)),
                  pl.BlockSpec((tq, LANES), lambda b, i: (b * nq + i, 0)),
                  pl.BlockSpec((tq, cfg.d_dsa), lambda b, i: (b * nq + i, qs_blk)),
                  pl.BlockSpec((t, LANES), lambda b, i: (b, 0)),
                  pl.BlockSpec((t, kvw), lambda b, i: (b, ks_blk)),
                  pl.BlockSpec((t, kvw), lambda b, i: (b, vs_blk))],
        out_specs=pl.BlockSpec((tq, cfg.d_dsa), lambda b, i: (b * nq + i, 0)),
        out_shape=jax.ShapeDtypeStruct((m, cfg.d_dsa), BF16),
        scratch_shapes=[pltpu.VMEM((t, LANES), BF16), pltpu.VMEM((t, kvw), BF16), pltpu.VMEM((t, kvw), BF16)],
        compiler_params=_params(("arbitrary", "arbitrary"), 56),
        name="dsa_prompt",
    )(p_p, kiwi_p, p_p, kiwi_p, p_p, p_p)


def _mix_out_kernel(x_ref, hx_ref, gb_ref, gc_ref, ha_ref, hb_ref, cw_ref, yd_ref, ys_ref, z_ref,
                    mod_ref, wo_ref, o_ref, ut_ref, mix_sc, *, tm, seq_tiles, d_conv, d_diff):
    i = pl.program_id(0)

    @pl.when(pl.program_id(1) == 0)
    def _():
        u = gc_ref[...] * hx_ref[...]
        uh = ha_ref[...] * hb_ref[...]
        if seq_tiles:
            uh = jnp.where(i % seq_tiles == 0, 0.0, uh)
        row = lax.broadcasted_iota(jnp.int32, (tm, d_conv), 0)
        u1 = jnp.where(row == 0, uh[7:8], pltpu.roll(u, 1, 0))
        u2 = jnp.where(row == 0, uh[6:7], jnp.where(row == 1, uh[7:8], pltpu.roll(u, 2, 0)))
        cw = cw_ref[...]
        yc = gb_ref[...] * (cw[0:1] * u2 + cw[1:2] * u1 + cw[2:3] * u)
        ut_ref[...] = u[tm - SUBLANES:tm]

        def gated(y, c0, c1):
            zf = z_ref[:, c0:c1].astype(F32)
            mix_sc[:, c0:c1] = (y * (zf * jax.nn.sigmoid(zf))).astype(BF16)

        gated(yc, 0, d_conv)
        gated(yd_ref[...].astype(F32), d_conv, d_conv + d_diff)
        gated(ys_ref[...].astype(F32), d_conv + d_diff, mix_sc.shape[1])

    o_ref[...] = x_ref[...] + mod_ref[2:3, :] * jnp.dot(mix_sc[...], wo_ref[...], preferred_element_type=F32)


def _mix_out_call(x2d, p_x, halo_a, halo_b, halo_map_a, halo_map_b, conv_w, y_diff, y_dsa, z_x, mod3,
                  wo16, layer, cfg, rows_per_batch, tm, seq_tiles, tn=512):
    m, d = x2d.shape
    dc = cfg.d_conv
    per = rows_per_batch // tm
    hx_blk, gb_blk, gc_blk = (cfg.off[n] // dc for n in ("hx", "gb", "gc"))
    kern = functools.partial(_mix_out_kernel, tm=tm, seq_tiles=seq_tiles, d_conv=dc, d_diff=cfg.d_diff)
    return pl.pallas_call(
        kern,
        grid=(m // tm, d // tn),
        in_specs=[pl.BlockSpec((tm, tn), lambda i, j: (i, j)),
                  pl.BlockSpec((tm, dc), lambda i, j: (i, hx_blk)),
                  pl.BlockSpec((tm, dc), lambda i, j: (i, gb_blk)),
                  pl.BlockSpec((tm, dc), lambda i, j: (i, gc_blk)),
                  pl.BlockSpec((SUBLANES, dc), halo_map_a),
                  pl.BlockSpec((SUBLANES, dc), halo_map_b),
                  pl.BlockSpec(conv_w.shape, lambda i, j: (0, 0)),
                  pl.BlockSpec((tm, cfg.d_diff), lambda i, j: (i, 0)),
                  pl.BlockSpec((tm, cfg.d_dsa), lambda i, j: (i, 0)),
                  pl.BlockSpec((tm, d), lambda i, j: (i, 0)),
                  pl.BlockSpec((None, 3, tn), lambda i, j: (i // per, 0, j)),
                  pl.BlockSpec((None, d, tn), lambda i, j: (layer, 0, j))],
        out_specs=[pl.BlockSpec((tm, tn), lambda i, j: (i, j)),
                   pl.BlockSpec((None, SUBLANES, dc), lambda i, j: (i, 0, 0))],
        out_shape=[jax.ShapeDtypeStruct((m, d), F32),
                   jax.ShapeDtypeStruct((m // tm, SUBLANES, dc), F32)],
        scratch_shapes=[pltpu.VMEM((tm, d), BF16)],
        compiler_params=_params(("arbitrary", "arbitrary"), 56),
        name="mix_out",
    )(x2d, p_x, p_x, p_x, halo_a, halo_b, conv_w, y_diff, y_dsa, z_x, mod3, wo16)


ATTN_PAGES = 16


def _page_specs(layer, pps, rows, width):
    def spec(p):
        return pl.BlockSpec((None, None, rows, width),
                            lambda b, s, pt: (layer, pt[b, s * pps + p], 0, 0))
    return [spec(p) for p in range(pps)]


def _cat_pages(refs, g, n_kv):
    return jnp.concatenate([r[pl.ds(g, PAGE_SIZE, stride=n_kv), :] for r in refs], axis=0).astype(BF16)


def _dec_diff_kernel(pt_ref, q_ref, *refs, pps, n_kv, rep, t_new, lam_init):
    kp, vp = refs[:pps], refs[pps:2 * pps]
    knew_ref, vnew_ref, lam_ref, sg_ref, o_ref, qs_sc, m_sc, l_sc, acc_sc = refs[2 * pps:]
    step = pl.program_id(1)
    rg = 2 * rep * SUBLANES
    groups = [slice(g * rg, (g + 1) * rg) for g in range(n_kv)]

    @pl.when(step == 0)
    def _():
        for h in range(n_kv * rep):
            qh = q_ref[:, h * HEAD_DIM:(h + 1) * HEAD_DIM] * ((HEAD_DIM // 2) ** -0.5)
            lo, hi = _half_masked(qh, SUBLANES)
            qs_sc[2 * h * SUBLANES:(2 * h + 1) * SUBLANES] = lo
            qs_sc[(2 * h + 1) * SUBLANES:(2 * h + 2) * SUBLANES] = hi
        _init_flash(m_sc, l_sc, acc_sc)

    scores = [_dot_nt(qs_sc[rows].astype(BF16), _cat_pages(kp, g, n_kv)) for g, rows in enumerate(groups)]
    for g, rows in enumerate(groups):
        _online_update(m_sc, l_sc, acc_sc, rows, scores[g], _cat_pages(vp, g, n_kv), False)

    @pl.when(step == pl.num_programs(1) - 1)
    def _():
        tok = lax.broadcasted_iota(jnp.int32, (rg, LANES), 0) % SUBLANES
        col = lax.broadcasted_iota(jnp.int32, (rg, LANES), 1)
        valid = (col <= tok) & (col < t_new)
        for g, rows in enumerate(groups):
            c0, c1 = g * HEAD_DIM, (g + 1) * HEAD_DIM
            s = jnp.where(valid, _dot_nt(qs_sc[rows].astype(BF16), knew_ref[:, c0:c1]), NEG_INF)
            _online_update(m_sc, l_sc, acc_sc, rows, s, vnew_ref[:, c0:c1], False)
        lmb = _lambda_full(lam_ref, lam_init)
        o_all = acc_sc[...] / l_sc[...]
        for h in range(n_kv * rep):
            o0 = o_all[2 * h * SUBLANES:(2 * h + 1) * SUBLANES]
            o1 = o_all[(2 * h + 1) * SUBLANES:(2 * h + 2) * SUBLANES]
            o_ref[:, h * HEAD_DIM:(h + 1) * HEAD_DIM] = _diff_head_out(o0, o1, lmb, sg_ref[...], lam_init)


def _flash_scratch(rows):
    return [pltpu.VMEM((rows, HEAD_DIM), F32), pltpu.VMEM((rows, LANES), F32),
            pltpu.VMEM((rows, LANES), F32), pltpu.VMEM((rows, HEAD_DIM), F32)]


def _dec_diff_call(page_table, p_s, cache_k, cache_v, knew16, vnew16, lam, sg, cfg, layer, t_new, lam_init):
    nb, n_pages = page_table.shape
    pps = math.gcd(n_pages, ATTN_PAGES)
    kvw = cfg.n_kv_diff * HEAD_DIM
    rows = cfg.n_kv_diff * 2 * cfg.rep_diff * SUBLANES
    q_blk = cfg.off["q_d"] // cfg.d_diff
    kern = functools.partial(_dec_diff_kernel, pps=pps, n_kv=cfg.n_kv_diff, rep=cfg.rep_diff,
                             t_new=t_new, lam_init=lam_init)
    grid_spec = pltpu.PrefetchScalarGridSpec(
        num_scalar_prefetch=1,
        grid=(nb, n_pages // pps),
        in_specs=[pl.BlockSpec((SUBLANES, cfg.d_diff), lambda b, s, pt: (b, q_blk))]
        + 2 * _page_specs(layer, pps, PAGE_SIZE * cfg.n_kv_diff, HEAD_DIM)
        + [pl.BlockSpec((None, PAGE_SIZE, kvw), lambda b, s, pt: (b, 0, 0)),
           pl.BlockSpec((None, PAGE_SIZE, kvw), lambda b, s, pt: (b, 0, 0)),
           pl.BlockSpec(lam.shape, lambda b, s, pt: (0, 0)),
           pl.BlockSpec((1, HEAD_DIM), lambda b, s, pt: (0, 0))],
        out_specs=pl.BlockSpec((SUBLANES, cfg.d_diff), lambda b, s, pt: (b, 0)),
        scratch_shapes=_flash_scratch(rows))
    return pl.pallas_call(
        kern,
        grid_spec=grid_spec,
        out_shape=jax.ShapeDtypeStruct((nb * SUBLANES, cfg.d_diff), F32),
        compiler_params=_params(("arbitrary", "arbitrary"), 48),
        name="diff_attn_sample",
    )(page_table, p_s, *([cache_k] * pps), *([cache_v] * pps), knew16, vnew16, lam, sg.reshape(1, HEAD_DIM))


def _dec_index_kernel(pt_ref, q_ref, w_ref, *refs, pps, k_top, t_new, past_len):
    kp = refs[:pps]
    knew_ref, bias_ref, sc_sc = refs[pps:]
    step = pl.program_id(1)
    span = pps * PAGE_SIZE

    def idx_score(s):
        s = jnp.maximum(s, 0.0) * (w_ref[...] * (N_IDX_HEADS ** -0.5))
        return jnp.sum(s.reshape(N_IDX_HEADS, SUBLANES, s.shape[-1]), axis=0)

    k_t = jnp.concatenate([r[...] for r in kp], axis=1).astype(BF16)
    off = pl.multiple_of(step * span, span)
    sc_sc[:, pl.ds(off, span)] = idx_score(jnp.dot(q_ref[...], k_t, preferred_element_type=F32))

    @pl.when(step == pl.num_programs(1) - 1)
    def _():
        tok = lax.broadcasted_iota(jnp.int32, (SUBLANES, LANES), 0)
        col = lax.broadcasted_iota(jnp.int32, (SUBLANES, LANES), 1)
        valid = (col <= tok) & (col < t_new)
        sc_sc[:, past_len:past_len + LANES] = jnp.where(valid, idx_score(_dot_nt(q_ref[...], knew_ref[...])), NEG_INF)
        bias_ref[...] = _selection_bias(sc_sc[...], k_top)


def _dec_index_call(page_table, qi_rows, wi_col, cache_kidx_t, kinew16, layer, t_new):
    nb, n_pages = page_table.shape
    pps = math.gcd(n_pages, ATTN_PAGES)
    past_len = n_pages * PAGE_SIZE
    k_top = min(TOPK_MAX, (past_len + t_new) // 4)
    w = past_len + LANES
    rows = N_IDX_HEADS * SUBLANES
    kern = functools.partial(_dec_index_kernel, pps=pps, k_top=k_top, t_new=t_new, past_len=past_len)
    grid_spec = pltpu.PrefetchScalarGridSpec(
        num_scalar_prefetch=1,
        grid=(nb, n_pages // pps),
        in_specs=[pl.BlockSpec((None, rows, D_IDX), lambda b, s, pt: (b, 0, 0)),
                  pl.BlockSpec((None, rows, 1), lambda b, s, pt: (b, 0, 0))]
        + _page_specs(layer, pps, D_IDX, PAGE_SIZE)
        + [pl.BlockSpec((None, PAGE_SIZE, D_IDX), lambda b, s, pt: (b, 0, 0))],
        out_specs=pl.BlockSpec((None, SUBLANES, w), lambda b, s, pt: (b, 0, 0)),
        scratch_shapes=[pltpu.VMEM((SUBLANES, w), F32)])
    return pl.pallas_call(
        kern,
        grid_spec=grid_spec,
        out_shape=jax.ShapeDtypeStruct((nb, SUBLANES, w), F32),
        compiler_params=_params(("arbitrary", "arbitrary"), 40),
        name="dsa_index_sample",
    )(page_table, qi_rows, wi_col, *([cache_kidx_t] * pps), kinew16)


def _dec_dsa_kernel(pt_ref, q_ref, bias_ref, bnew_ref, *refs, pps, n_kv, rep):
    kp, vp = refs[:pps], refs[pps:2 * pps]
    knew_ref, vnew_ref, o_ref, qs_sc, m_sc, l_sc, acc_sc = refs[2 * pps:]
    step = pl.program_id(1)
    rg = rep * SUBLANES
    groups = [slice(g * rg, (g + 1) * rg) for g in range(n_kv)]
    scale = HEAD_DIM ** -0.5

    @pl.when(step == 0)
    def _():
        for h in range(n_kv * rep):
            qs_sc[h * SUBLANES:(h + 1) * SUBLANES] = q_ref[:, h * HEAD_DIM:(h + 1) * HEAD_DIM]
        _init_flash(m_sc, l_sc, acc_sc)

    def scores(g, k16, bias):
        s = _dot_nt(qs_sc[groups[g]].astype(BF16), k16) * scale
        n = s.shape[-1]
        return (s.reshape(rep, SUBLANES, n) + bias[None]).reshape(rg, n)

    s_all = [scores(g, _cat_pages(kp, g, n_kv), bias_ref[...]) for g in range(n_kv)]
    for g in range(n_kv):
        _online_update(m_sc, l_sc, acc_sc, groups[g], s_all[g], _cat_pages(vp, g, n_kv), True)

    @pl.when(step == pl.num_programs(1) - 1)
    def _():
        for g in range(n_kv):
            c0, c1 = g * HEAD_DIM, (g + 1) * HEAD_DIM
            _online_update(m_sc, l_sc, acc_sc, groups[g], scores(g, knew_ref[:, c0:c1], bnew_ref[...]),
                           vnew_ref[:, c0:c1], True)
        o_all = acc_sc[...] / l_sc[...]
        for h in range(n_kv * rep):
            o_ref[:, h * HEAD_DIM:(h + 1) * HEAD_DIM] = o_all[h * SUBLANES:(h + 1) * SUBLANES]


def _dec_dsa_call(page_table, p_s, bias, cache_k, cache_v, knew16, vnew16, cfg, layer):
    nb, n_pages = page_table.shape
    pps = math.gcd(n_pages, ATTN_PAGES)
    kvw = cfg.n_kv_dsa * HEAD_DIM
    rows = cfg.n_kv_dsa * cfg.rep_dsa * SUBLANES
    span = pps * PAGE_SIZE
    new_blk = n_pages * PAGE_SIZE // LANES
    q_blk = cfg.off["q_s"] // cfg.d_dsa
    kern = functools.partial(_dec_dsa_kernel, pps=pps, n_kv=cfg.n_kv_dsa, rep=cfg.rep_dsa)
    grid_spec = pltpu.PrefetchScalarGridSpec(
        num_scalar_prefetch=1,
        grid=(nb, n_pages // pps),
        in_specs=[pl.BlockSpec((SUBLANES, cfg.d_dsa), lambda b, s, pt: (b, q_blk)),
                  pl.BlockSpec((None, SUBLANES, span), lambda b, s, pt: (b, 0, s)),
                  pl.BlockSpec((None, SUBLANES, LANES), lambda b, s, pt: (b, 0, new_blk))]
        + 2 * _page_specs(layer, pps, PAGE_SIZE * cfg.n_kv_dsa, HEAD_DIM)
        + [pl.BlockSpec((None, PAGE_SIZE, kvw), lambda b, s, pt: (b, 0, 0)),
           pl.BlockSpec((None, PAGE_SIZE, kvw), lambda b, s, pt: (b, 0, 0))],
        out_specs=pl.BlockSpec((SUBLANES, cfg.d_dsa), lambda b, s, pt: (b, 0)),
        scratch_shapes=_flash_scratch(rows))
    return pl.pallas_call(
        kern,
        grid_spec=grid_spec,
        out_shape=jax.ShapeDtypeStruct((nb * SUBLANES, cfg.d_dsa), F32),
        compiler_params=_params(("arbitrary", "arbitrary"), 40),
        name="dsa_attn_sample",
    )(page_table, p_s, bias, bias, *([cache_k] * pps), *([cache_v] * pps), knew16, vnew16)


def _pad_rows(a, rows):
    return jnp.pad(a, ((0, 0), (0, rows - a.shape[1]), (0, 0)))


def kernel(x_prompt, x_sample, cache_diff_k, cache_diff_v, cache_dsa_k, cache_dsa_v, cache_dsa_kidx,
           state_conv, page_table, c_prompt, c_sample, norm_g, w_ada, b_ada, w_in, conv_w,
           diff_lam, diff_subln_g, w_out, final_g):
    nbp, t, d = x_prompt.shape
    nbs, t_new, _ = x_sample.shape
    depth, n_pool = cache_diff_k.shape[:2]
    cfg = _make_cfg(d, cache_diff_k.shape[3], cache_dsa_k.shape[3])
    dc = cfg.d_conv
    off = cfg.off
    kdw, ksw = cfg.n_kv_diff * HEAD_DIM, cfg.n_kv_dsa * HEAD_DIM
    kiwi_w = D_IDX + N_IDX_HEADS
    assert cache_diff_k.shape[2] == PAGE_SIZE and cache_diff_k.shape[4] == HEAD_DIM
    assert state_conv.shape[2] == CONV_WIDTH - 1 and CONV_WIDTH - 1 <= t_new <= SUBLANES
    assert w_in.shape[2] == cfg.n_main + kiwi_w + d and cfg.d_diff == cfg.d_dsa

    w_main16 = w_in.astype(BF16)
    w_kiwi16 = jnp.pad(w_in[:, :, cfg.n_main:cfg.n_main + kiwi_w], ((0, 0), (0, 0), (0, LANES - kiwi_w))).astype(BF16)
    w_z16 = w_in[:, :, cfg.n_main + kiwi_w:].astype(BF16)
    wo16 = w_out.astype(BF16)
    z_perm = tuple(range(d // PROJ_TN))

    n_c = nbp + nbs
    c_all = _pad_rows(jnp.concatenate([c_prompt, c_sample], axis=0)[None], -(-n_c // SUBLANES) * SUBLANES)[0]
    mod = _mod_call(c_all, w_ada, b_ada).reshape(depth, c_all.shape[0], 3, d)

    ck_d = cache_diff_k.reshape(depth, n_pool, PAGE_SIZE * cfg.n_kv_diff, HEAD_DIM)
    cv_d = cache_diff_v.reshape(depth, n_pool, PAGE_SIZE * cfg.n_kv_diff, HEAD_DIM)
    ck_s = cache_dsa_k.reshape(depth, n_pool, PAGE_SIZE * cfg.n_kv_dsa, HEAD_DIM)
    cv_s = cache_dsa_v.reshape(depth, n_pool, PAGE_SIZE * cfg.n_kv_dsa, HEAD_DIM)
    ck_i = jnp.swapaxes(cache_dsa_kidx, 2, 3)

    xp = x_prompt.reshape(nbp * t, d)
    xs = _pad_rows(x_sample, SUBLANES).reshape(nbs * SUBLANES, d)
    ones_halo = jnp.ones((SUBLANES, dc), F32)
    tm_p = 512
    seq_tiles = t // tm_p
    halo_blocks = tm_p // SUBLANES
    ms = nbs * SUBLANES

    def prompt_halo(name):
        col_blk = off[name] // dc
        return lambda i, j: (jnp.maximum(i * halo_blocks - 1, 0), col_blk)

    def project(h16, l, tm, z_dtype):
        p = _matmul_call(h16, w_main16, l, cfg.perm, F32, tm, PROJ_TN)
        z = _matmul_call(h16, w_z16, l, z_perm, z_dtype, tm, PROJ_TN)
        kiwi = _matmul_call(h16, w_kiwi16, l, (0,), F32, tm, LANES)
        return p, z, kiwi

    p_rows, s_rows = [], []
    for l in range(depth):
        lam_init = LAMBDA_INIT_BASE - LAMBDA_INIT_AMP * math.exp(-LAMBDA_INIT_RATE * l)
        mod_p, mod_s = mod[l, :nbp], mod[l, nbp:n_c]

        h_p = _norm_mod_call(xp, norm_g[l], mod_p, t, 256, BF16)
        p_p, z_p, kiwi_p = project(h_p, l, 1024, BF16)
        yd_p = _diff_prompt_call(p_p, diff_lam[l], diff_subln_g[l], cfg, nbp, t, lam_init)
        ys_p = _dsa_prompt_call(p_p, kiwi_p, cfg, nbp, t)
        xp, ut_p = _mix_out_call(xp, p_p, p_p, p_p, prompt_halo("hx"), prompt_halo("gc"), conv_w[l], yd_p, ys_p,
                                 z_p, mod_p, wo16, l, cfg, t, tm_p, seq_tiles)
        p_rows.append((p_p, kiwi_p,
                       ut_p.reshape(nbp, seq_tiles, SUBLANES, dc)[:, -1, SUBLANES - (CONV_WIDTH - 1):]))

        h_s = _norm_mod_call(xs, norm_g[l], mod_s, SUBLANES, SUBLANES, F32).astype(BF16)
        p_s, z_s, kiwi_s = project(h_s, l, ms, F32)
        p_s3 = p_s.reshape(nbs, SUBLANES, cfg.n_main)
        kiwi_s3 = kiwi_s.reshape(nbs, SUBLANES, LANES)

        def new_rows(name, width):
            return _pad_rows(p_s3[:, :, off[name]:off[name] + width], PAGE_SIZE).astype(BF16)

        yd_s = _dec_diff_call(page_table, p_s, ck_d, cv_d, new_rows("k_d", kdw), new_rows("v_d", kdw),
                              diff_lam[l], diff_subln_g[l], cfg, l, t_new, lam_init)
        qi_rows = p_s3[:, :, off["qi"]:off["qi"] + N_IDX_HEADS * D_IDX].reshape(nbs, SUBLANES, N_IDX_HEADS, D_IDX)
        qi_rows = qi_rows.transpose(0, 2, 1, 3).reshape(nbs, N_IDX_HEADS * SUBLANES, D_IDX).astype(BF16)
        wi_col = kiwi_s3[:, :, D_IDX:kiwi_w].transpose(0, 2, 1).reshape(nbs, N_IDX_HEADS * SUBLANES, 1)
        ki_new = _pad_rows(kiwi_s3[:, :, :D_IDX], PAGE_SIZE).astype(BF16)
        bias = _dec_index_call(page_table, qi_rows, wi_col, ck_i, ki_new, l, t_new)
        ys_s = _dec_dsa_call(page_table, p_s, bias, ck_s, cv_s, new_rows("k_s", ksw), new_rows("v_s", ksw),
                             cfg, l)
        state_pad = jnp.pad(state_conv[l], ((0, 0), (SUBLANES - (CONV_WIDTH - 1), 0), (0, 0)))
        xs, ut_s = _mix_out_call(xs, p_s, state_pad.reshape(ms, dc), ones_halo,
                                 lambda i, j: (i, 0), lambda i, j: (0, 0), conv_w[l], yd_s, ys_s,
                                 z_s, mod_s, wo16, l, cfg, SUBLANES, SUBLANES, 0)
        s_rows.append((p_s3[:, :t_new], kiwi_s3[:, :t_new], ut_s[:, t_new - (CONV_WIDTH - 1):t_new]))

    y_prompt = _rmsnorm_call(xp, final_g, 256).reshape(nbp, t, d)
    y_sample = _rmsnorm_call(xs, final_g, SUBLANES).reshape(nbs, SUBLANES, d)[:, :t_new]

    def stack_p(name, n_kv):
        return jnp.stack([r[0][:, off[name]:off[name] + n_kv * HEAD_DIM].reshape(nbp, t, n_kv, HEAD_DIM)
                          for r in p_rows])

    def stack_s(name, n_kv):
        return jnp.stack([r[0][:, :, off[name]:off[name] + n_kv * HEAD_DIM].reshape(nbs, t_new, n_kv, HEAD_DIM)
                          for r in s_rows])

    return (y_prompt, y_sample,
            stack_p("k_d", cfg.n_kv_diff), stack_p("v_d", cfg.n_kv_diff),
            stack_p("k_s", cfg.n_kv_dsa), stack_p("v_s", cfg.n_kv_dsa),
            jnp.stack([r[1][:, :D_IDX].reshape(nbp, t, D_IDX) for r in p_rows]),
            jnp.stack([r[2] for r in p_rows]),
            stack_s("k_d", cfg.n_kv_diff), stack_s("v_d", cfg.n_kv_diff),
            stack_s("k_s", cfg.n_kv_dsa), stack_s("v_s", cfg.n_kv_dsa),
            jnp.stack([r[1][:, :, :D_IDX] for r in s_rows]),
            jnp.stack([r[2] for r in s_rows]))
```

```python
import functools
import math
from typing import NamedTuple

import jax
import jax.numpy as jnp
from jax import lax
from jax.experimental import pallas as pl
from jax.experimental.pallas import tpu as pltpu

F32 = jnp.float32
BF16 = jnp.bfloat16

HEAD_DIM = 128
D_IDX = 64
N_IDX_HEADS = 16
CONV_WIDTH = 3
TOPK_MAX = 256
PAGE_SIZE = 128
RMS_EPS = 1e-6
LAMBDA_INIT_BASE = 0.8
LAMBDA_INIT_AMP = 0.6
LAMBDA_INIT_RATE = 0.3

LANES = 128
SUBLANES = 8
MIB = 2 ** 20
PROJ_TN = 512

NEG_INF = float("-inf")
INT_MIN = -(2 ** 31)
KEY_NEG_INF = INT_MIN + 0x7FFFFF


class Cfg(NamedTuple):
    d_model: int
    d_conv: int
    n_kv_diff: int
    rep_diff: int
    d_diff: int
    n_kv_dsa: int
    rep_dsa: int
    d_dsa: int
    n_main: int
    off: dict
    perm: tuple


def _make_cfg(d_model, n_kv_diff, n_kv_dsa):
    d_conv = d_model // 4
    d_diff = 3 * d_model // 8
    d_dsa = 3 * d_model // 8
    kd, ks = n_kv_diff * HEAD_DIM, n_kv_dsa * HEAD_DIM
    qi = N_IDX_HEADS * D_IDX
    w_order = (("hx", d_conv), ("gb", d_conv), ("gc", d_conv), ("q_d", d_diff), ("k_d", kd), ("v_d", kd),
               ("q_s", d_dsa), ("kv_s", 2 * ks), ("qi", qi))
    out_order = ("q_d", "q_s", "qi", "hx", "gb", "gc", "k_d", "v_d", "kv_s")
    width = dict(w_order)
    off, pos = {}, 0
    for name in out_order:
        assert width[name] % PROJ_TN == 0
        off[name] = pos
        pos += width[name]
    off["k_s"], off["v_s"] = off["kv_s"], off["kv_s"] + ks
    perm = []
    for name, wd in w_order:
        perm.extend(off[name] // PROJ_TN + b for b in range(wd // PROJ_TN))
    return Cfg(d_model, d_conv, n_kv_diff, d_diff // HEAD_DIM // n_kv_diff, d_diff,
               n_kv_dsa, d_dsa // HEAD_DIM // n_kv_dsa, d_dsa, pos, off, tuple(perm))


def _params(semantics, vmem_mib):
    return pltpu.CompilerParams(dimension_semantics=semantics, vmem_limit_bytes=vmem_mib * MIB)


def _dot_nt(a, b):
    return lax.dot_general(a, b, (((1,), (1,)), ((), ())), preferred_element_type=F32)


MOD_STREAMS = 8
MOD_ROWS = 32


def _mod_kernel(c_ref, *refs):
    w_refs, (b_ref, o_ref) = refs[:MOD_STREAMS], refs[MOD_STREAMS:]

    @pl.when(pl.program_id(1) == 0)
    def _():
        o_ref[...] = jnp.broadcast_to(b_ref[...], o_ref.shape)

    c = c_ref[...]
    a = (c * jax.nn.sigmoid(c)).astype(BF16)
    w = jnp.concatenate([r[...].astype(BF16) for r in w_refs], axis=0)
    o_ref[...] += jnp.dot(a, w, preferred_element_type=F32)


def _mod_call(c_all, w_ada, b_ada):
    depth, d, n3 = w_ada.shape
    rows = c_all.shape[0]
    tk = MOD_STREAMS * MOD_ROWS

    def w_spec(p):
        return pl.BlockSpec((None, MOD_ROWS, n3), lambda l, k: (l, k * MOD_STREAMS + p, 0))

    return pl.pallas_call(
        _mod_kernel,
        grid=(depth, d // tk),
        in_specs=[pl.BlockSpec((rows, tk), lambda l, k: (0, k))]
        + [w_spec(p) for p in range(MOD_STREAMS)]
        + [pl.BlockSpec((None, 1, n3), lambda l, k: (l, 0, 0))],
        out_specs=pl.BlockSpec((None, rows, n3), lambda l, k: (l, 0, 0)),
        out_shape=jax.ShapeDtypeStruct((depth, rows, n3), F32),
        compiler_params=_params(("arbitrary", "arbitrary"), 48),
        name="adaln_mod",
    )(c_all, *([w_ada] * MOD_STREAMS), b_ada.reshape(depth, 1, n3))


def _rms(x):
    return x * lax.rsqrt(jnp.mean(x * x, axis=-1, keepdims=True) + RMS_EPS)


def _norm_mod_kernel(x_ref, g_ref, mod_ref, o_ref):
    y = _rms(x_ref[...]) * g_ref[...]
    o_ref[...] = (y * (1.0 + mod_ref[1:2, :]) + mod_ref[0:1, :]).astype(o_ref.dtype)


def _norm_mod_call(x2d, g, mod3, rows_per_batch, tm, out_dtype):
    m, d = x2d.shape
    per = rows_per_batch // tm
    return pl.pallas_call(
        _norm_mod_kernel,
        grid=(m // tm,),
        in_specs=[pl.BlockSpec((tm, d), lambda i: (i, 0)),
                  pl.BlockSpec((1, d), lambda i: (0, 0)),
                  pl.BlockSpec((None, 3, d), lambda i: (i // per, 0, 0))],
        out_specs=pl.BlockSpec((tm, d), lambda i: (i, 0)),
        out_shape=jax.ShapeDtypeStruct((m, d), out_dtype),
        compiler_params=_params(("arbitrary",), 40),
        name="norm_mod",
    )(x2d, g.reshape(1, d), mod3)


def _rmsnorm_kernel(x_ref, g_ref, o_ref):
    o_ref[...] = _rms(x_ref[...]) * g_ref[...]


def _rmsnorm_call(x2d, g, tm):
    m, d = x2d.shape
    return pl.pallas_call(
        _rmsnorm_kernel,
        grid=(m // tm,),
        in_specs=[pl.BlockSpec((tm, d), lambda i: (i, 0)),
                  pl.BlockSpec((1, d), lambda i: (0, 0))],
        out_specs=pl.BlockSpec((tm, d), lambda i: (i, 0)),
        out_shape=jax.ShapeDtypeStruct((m, d), F32),
        compiler_params=_params(("arbitrary",), 40),
        name="final_norm",
    )(x2d, g.reshape(1, d))


def _matmul_kernel(perm_ref, x_ref, w_ref, o_ref):
    o_ref[...] = jnp.dot(x_ref[...], w_ref[...], preferred_element_type=F32).astype(o_ref.dtype)


def _matmul_call(x16, w16, layer, perm, out_dtype, tm, tn):
    m, k = x16.shape
    n_blocks = len(perm)
    grid_spec = pltpu.PrefetchScalarGridSpec(
        num_scalar_prefetch=1,
        grid=(m // tm, n_blocks),
        in_specs=[pl.BlockSpec((tm, k), lambda i, j, pm: (i, 0)),
                  pl.BlockSpec((None, k, tn), lambda i, j, pm: (layer, 0, j))],
        out_specs=pl.BlockSpec((tm, tn), lambda i, j, pm: (i, pm[j])))
    return pl.pallas_call(
        _matmul_kernel,
        grid_spec=grid_spec,
        out_shape=jax.ShapeDtypeStruct((m, n_blocks * tn), out_dtype),
        compiler_params=_params(("arbitrary", "arbitrary"), 48),
        name="in_proj",
    )(jnp.asarray(perm, jnp.int32), x16, w16)


def _half_masked(q, rows):
    lane = lax.broadcasted_iota(jnp.int32, (rows, HEAD_DIM), 1)
    zero = jnp.zeros_like(q)
    return jnp.where(lane < HEAD_DIM // 2, q, zero), jnp.where(lane >= HEAD_DIM // 2, q, zero)


def _lambda_full(lam_ref, lam_init):
    lam = lam_ref[...]
    a = jnp.sum(lam[0:1, :] * lam[1:2, :], axis=-1, keepdims=True)
    b = jnp.sum(lam[2:3, :] * lam[3:4, :], axis=-1, keepdims=True)
    return jnp.exp(a) - jnp.exp(b) + lam_init


def _diff_head_out(o0, o1, lmb, sg, lam_init):
    o = o0 - lmb * o1
    return _rms(o) * sg * (1.0 - lam_init)


def _online_update(m_sc, l_sc, acc_sc, rows, s, v16, guard_empty):
    m_prev = m_sc[rows]
    m_new = jnp.maximum(m_prev, jnp.max(s, axis=-1, keepdims=True))
    m_use = jnp.where(m_new == NEG_INF, 0.0, m_new) if guard_empty else m_new
    alpha = jnp.exp(m_prev - m_use)
    p = jnp.exp(s - jnp.concatenate([m_use] * (s.shape[-1] // LANES), axis=1))
    l_sc[rows] = alpha * l_sc[rows] + jnp.sum(p, axis=-1, keepdims=True)
    acc_sc[rows] = alpha * acc_sc[rows] + jnp.dot(p.astype(BF16), v16, preferred_element_type=F32)
    m_sc[rows] = m_new


def _init_flash(m_sc, l_sc, acc_sc):
    m_sc[...] = jnp.full_like(m_sc, NEG_INF)
    l_sc[...] = jnp.zeros_like(l_sc)
    acc_sc[...] = jnp.zeros_like(acc_sc)


def _diff_prompt_kernel(q_ref, k_ref, v_ref, lam_ref, sg_ref, o_ref, qs_sc, m_sc, l_sc, acc_sc,
                        *, rep, tq, lam_init):
    i = pl.program_id(2)
    chunk = LANES
    n_chunks = 2 * rep * tq // chunk
    for r in range(rep):
        qr = q_ref[:, r * HEAD_DIM:(r + 1) * HEAD_DIM] * ((HEAD_DIM // 2) ** -0.5)
        lo, hi = _half_masked(qr, tq)
        qs_sc[(2 * r) * tq:(2 * r + 1) * tq] = lo.astype(BF16)
        qs_sc[(2 * r + 1) * tq:(2 * r + 2) * tq] = hi.astype(BF16)
    _init_flash(m_sc, l_sc, acc_sc)

    def kv_block(j, diagonal):
        start = pl.multiple_of(j * tq, tq)
        k = k_ref[pl.ds(start, tq), :].astype(BF16)
        v = v_ref[pl.ds(start, tq), :].astype(BF16)
        for c in range(n_chunks):
            rows = slice(c * chunk, (c + 1) * chunk)
            s = _dot_nt(qs_sc[rows], k)
            if diagonal:
                q0 = (c * chunk) % tq
                row = lax.broadcasted_iota(jnp.int32, (chunk, tq), 0) + q0
                col = lax.broadcasted_iota(jnp.int32, (chunk, tq), 1)
                s = jnp.where(col <= row, s, NEG_INF)
            _online_update(m_sc, l_sc, acc_sc, rows, s, v, False)

    def full_block(j, carry):
        kv_block(j, False)
        return carry

    lax.fori_loop(0, i, full_block, 0)
    kv_block(i, True)

    lmb = _lambda_full(lam_ref, lam_init)
    for r in range(rep):
        lo = slice((2 * r) * tq, (2 * r + 1) * tq)
        hi = slice((2 * r + 1) * tq, (2 * r + 2) * tq)
        y = _diff_head_out(acc_sc[lo] / l_sc[lo], acc_sc[hi] / l_sc[hi], lmb, sg_ref[...], lam_init)
        o_ref[:, r * HEAD_DIM:(r + 1) * HEAD_DIM] = y.astype(BF16)


def _diff_prompt_call(p_p, lam, sg, cfg, nb, t, lam_init, tq=256):
    m = p_p.shape[0]
    nq = t // tq
    g_w = cfg.rep_diff * HEAD_DIM
    rows = 2 * cfg.rep_diff * tq
    q_blk, k_blk, v_blk = cfg.off["q_d"] // g_w, cfg.off["k_d"] // HEAD_DIM, cfg.off["v_d"] // HEAD_DIM
    kern = functools.partial(_diff_prompt_kernel, rep=cfg.rep_diff, tq=tq, lam_init=lam_init)
    return pl.pallas_call(
        kern,
        grid=(nb, cfg.n_kv_diff, nq),
        in_specs=[pl.BlockSpec((tq, g_w), lambda b, g, i: (b * nq + i, q_blk + g)),
                  pl.BlockSpec((t, HEAD_DIM), lambda b, g, i: (b, k_blk + g)),
                  pl.BlockSpec((t, HEAD_DIM), lambda b, g, i: (b, v_blk + g)),
                  pl.BlockSpec(lam.shape, lambda b, g, i: (0, 0)),
                  pl.BlockSpec((1, HEAD_DIM), lambda b, g, i: (0, 0))],
        out_specs=pl.BlockSpec((tq, g_w), lambda b, g, i: (b * nq + i, g)),
        out_shape=jax.ShapeDtypeStruct((m, cfg.d_diff), BF16),
        scratch_shapes=[pltpu.VMEM((rows, HEAD_DIM), BF16), pltpu.VMEM((rows, LANES), F32),
                        pltpu.VMEM((rows, LANES), F32), pltpu.VMEM((rows, HEAD_DIM), F32)],
        compiler_params=_params(("arbitrary", "arbitrary", "arbitrary"), 40),
        name="diff_attn_prompt",
    )(p_p, p_p, p_p, lam, sg.reshape(1, HEAD_DIM))


def _monotone_key(x):
    bits = lax.bitcast_convert_type(x, jnp.int32)
    key = bits ^ ((bits >> 31) & jnp.int32(0x7FFFFFFF))
    return jnp.where(bits == jnp.int32(INT_MIN), 0, key)


def _count(mask):
    return jnp.sum(jnp.where(mask, 1.0, 0.0), axis=-1, keepdims=True)


def _topk_mask(score, k_top):
    r, w = score.shape
    key = _monotone_key(score)
    kf = float(k_top)

    def value_bit(it, t):
        cand = t + (jnp.int32(1) << (31 - it))
        return jnp.where(_count(key >= cand) >= kf, cand, t)

    t = lax.fori_loop(0, 32, value_bit, jnp.full((r, 1), INT_MIN, jnp.int32))
    gt = key > t
    eq = key == t
    n_gt = _count(gt)
    need = kf - n_gt
    col = lax.broadcasted_iota(jnp.int32, (r, w), 1)
    ambiguous = (n_gt + _count(eq) > kf) & (t > KEY_NEG_INF)
    n_bits = w.bit_length()

    def tie_search(_):
        def index_bit(it, j):
            cand = j + (jnp.int32(1) << (n_bits - 1 - it))
            return jnp.where(_count(eq & (col < cand)) < need, cand, j)
        return lax.fori_loop(0, n_bits, index_bit, jnp.zeros((r, 1), jnp.int32))

    any_amb = jnp.max(jnp.where(ambiguous, 1.0, 0.0)) > 0.5
    j = lax.cond(any_amb, tie_search, lambda _: jnp.full((r, 1), w, jnp.int32), 0)
    return gt | (eq & (col <= j))


def _selection_bias(score, k_top):
    sel = _topk_mask(score, k_top) & (score > NEG_INF)
    return jnp.where(sel, 0.0, NEG_INF)


DSA_UNIT = 128
DSA_UNITS = 2


def _dsa_prompt_kernel(qi_ref, wi_ref, qs_ref, ki_ref, ks_ref, vs_ref, o_ref, kiki_sc, ks_sc, vs_sc,
                       score_sc, bias_sc, *, widths, steps_per_width, k_top, n_kv, rep):
    i = pl.program_id(1)

    @pl.when(i == 0)
    def _():
        ki = ki_ref[...]
        lane = lax.broadcasted_iota(jnp.int32, ki.shape, 1)
        kiki_sc[...] = jnp.where(lane < D_IDX, ki, pltpu.roll(ki, D_IDX, 1)).astype(BF16)
        ks_sc[...] = ks_ref[...].astype(BF16)
        vs_sc[...] = vs_ref[...].astype(BF16)

    for c, w in enumerate(widths):
        pl.when(i // steps_per_width == c)(
            functools.partial(_dsa_prompt_body, qi_ref, wi_ref, qs_ref, kiki_sc, ks_sc, vs_sc, o_ref, score_sc,
                              bias_sc, i=i, w=w, k_top=k_top, n_kv=n_kv, rep=rep))


def _dsa_prompt_body(qi_ref, wi_ref, qs_ref, kiki_sc, ks_sc, vs_sc, o_ref, score_sc, bias_sc,
                     *, i, w, k_top, n_kv, rep):
    tq = DSA_UNIT
    first_row = i * (DSA_UNITS * tq)
    kiki = kiki_sc[0:w, :]

    def index_pass(u, carry):
        r0 = pl.multiple_of(u * tq, tq)
        rows = pl.ds(r0, tq)
        wi = wi_ref[rows, :][:, D_IDX:D_IDX + N_IDX_HEADS] * (N_IDX_HEADS ** -0.5)
        idx = jnp.zeros((tq, w), F32)
        for j in range(N_IDX_HEADS // 2):
            lo, hi = _half_masked(qi_ref[rows, j * LANES:(j + 1) * LANES], tq)
            sc = jnp.maximum(_dot_nt(jnp.concatenate([lo, hi], axis=0).astype(BF16), kiki), 0.0)
            idx = idx + sc[:tq] * wi[:, 2 * j:2 * j + 1] + sc[tq:] * wi[:, 2 * j + 1:2 * j + 2]
        row = lax.broadcasted_iota(jnp.int32, (tq, w), 0) + (first_row + r0)
        col = lax.broadcasted_iota(jnp.int32, (tq, w), 1)
        score_sc[rows, 0:w] = jnp.where(col <= row, idx, NEG_INF)
        return carry

    lax.fori_loop(0, DSA_UNITS, index_pass, 0)
    bias_sc[:, 0:w] = _selection_bias(score_sc[:, 0:w], k_top)

    def attend_pass(u, carry):
        rows = pl.ds(pl.multiple_of(u * tq, tq), tq)
        bias = bias_sc[rows, 0:w]
        for g in range(n_kv):
            heads = [qs_ref[rows, (g * rep + r) * HEAD_DIM:(g * rep + r + 1) * HEAD_DIM] for r in range(rep)]
            q = jnp.concatenate(heads, axis=0).astype(BF16)
            s = _dot_nt(q, ks_sc[0:w, g * HEAD_DIM:(g + 1) * HEAD_DIM]) * (HEAD_DIM ** -0.5)
            s = s.reshape(rep, tq, w) + bias[None]
            p = jnp.exp(s - jnp.max(s, axis=-1, keepdims=True))
            l = jnp.sum(p, axis=-1, keepdims=True).reshape(rep * tq, 1)
            o = jnp.dot(p.reshape(rep * tq, w).astype(BF16), vs_sc[0:w, g * HEAD_DIM:(g + 1) * HEAD_DIM],
                        preferred_element_type=F32) / l
            for r in range(rep):
                o_ref[rows, (g * rep + r) * HEAD_DIM:(g * rep + r + 1) * HEAD_DIM] = o[r * tq:(r + 1) * tq].astype(BF16)
        return carry

    lax.fori_loop(0, DSA_UNITS, attend_pass, 0)


def _dsa_prompt_call(p_p, kiwi_p, cfg, nb, t):
    m = p_p.shape[0]
    tb = DSA_UNITS * DSA_UNIT
    n_steps = t // tb
    k_top = min(TOPK_MAX, t // 4)
    qi_w = N_IDX_HEADS * D_IDX
    kvw = cfg.n_kv_dsa * HEAD_DIM
    qi_blk, qs_blk = cfg.off["qi"] // qi_w, cfg.off["q_s"] // cfg.d_dsa
    ks_blk, vs_blk = cfg.off["k_s"] // kvw, cfg.off["v_s"] // kvw
    n_widths = math.gcd(n_steps, 4)
    widths = tuple(t * (c + 1) // n_widths for c in range(n_widths))
    kern = functools.partial(_dsa_prompt_kernel, widths=widths, steps_per_width=n_steps // n_widths, k_top=k_top,
                             n_kv=cfg.n_kv_dsa, rep=cfg.rep_dsa)
    return pl.pallas_call(
        kern,
        grid=(nb, n_steps),
        in_specs=[pl.BlockSpec((tb, qi_w), lambda b, i: (b * n_steps + i, qi_blk)),
                  pl.BlockSpec((tb, LANES), lambda b, i: (b * n_steps + i, 0)),
                  pl.BlockSpec((tb, cfg.d_dsa), lambda b, i: (b * n_steps + i, qs_blk)),
                  pl.BlockSpec((t, LANES), lambda b, i: (b, 0)),
                  pl.BlockSpec((t, kvw), lambda b, i: (b, ks_blk)),
                  pl.BlockSpec((t, kvw), lambda b, i: (b, vs_blk))],
        out_specs=pl.BlockSpec((tb, cfg.d_dsa), lambda b, i: (b * n_steps + i, 0)),
        out_shape=jax.ShapeDtypeStruct((m, cfg.d_dsa), BF16),
        scratch_shapes=[pltpu.VMEM((t, LANES), BF16), pltpu.VMEM((t, kvw), BF16), pltpu.VMEM((t, kvw), BF16),
                        pltpu.VMEM((tb, t), F32), pltpu.VMEM((tb, t), F32)],
        compiler_params=_params(("arbitrary", "arbitrary"), 56),
        name="dsa_prompt",
    )(p_p, kiwi_p, p_p, kiwi_p, p_p, p_p)


def _mix_out_kernel(x_ref, hx_ref, gb_ref, gc_ref, ha_ref, hb_ref, cw_ref, yd_ref, ys_ref, z_ref,
                    mod_ref, wo_ref, o_ref, ut_ref, mix_sc, *, tm, seq_tiles, d_conv, d_diff, cols_outer):
    i = pl.program_id(1 if cols_outer else 0)
    j = pl.program_id(0 if cols_outer else 1)
    slot = i if cols_outer else 0
    if cols_outer:
        ut_ref[...] = (gc_ref[...] * hx_ref[...])[tm - SUBLANES:tm]

    @pl.when(j == 0)
    def _():
        u = gc_ref[...] * hx_ref[...]
        uh = ha_ref[...] * hb_ref[...]
        if seq_tiles:
            uh = jnp.where(i % seq_tiles == 0, 0.0, uh)
        row = lax.broadcasted_iota(jnp.int32, (tm, d_conv), 0)
        u1 = jnp.where(row == 0, uh[7:8], pltpu.roll(u, 1, 0))
        u2 = jnp.where(row == 0, uh[6:7], jnp.where(row == 1, uh[7:8], pltpu.roll(u, 2, 0)))
        cw = cw_ref[...]
        yc = gb_ref[...] * (cw[0:1] * u2 + cw[1:2] * u1 + cw[2:3] * u)
        if not cols_outer:
            ut_ref[...] = u[tm - SUBLANES:tm]

        def gated(y, c0, c1):
            zf = z_ref[:, c0:c1].astype(F32)
            mix_sc[slot, :, c0:c1] = (y * (zf * jax.nn.sigmoid(zf))).astype(mix_sc.dtype)

        gated(yc, 0, d_conv)
        gated(yd_ref[...].astype(F32), d_conv, d_conv + d_diff)
        gated(ys_ref[...].astype(F32), d_conv + d_diff, mix_sc.shape[2])

    mix = mix_sc[slot].astype(BF16)
    o_ref[...] = x_ref[...] + mod_ref[2:3, :] * jnp.dot(mix, wo_ref[...], preferred_element_type=F32)


def _mix_out_call(x2d, p_x, halo_a, halo_b, halo_map_a, halo_map_b, conv_w, y_diff, y_dsa, z_x, mod3,
                  wo16, layer, cfg, rows_per_batch, tm, seq_tiles, cols_outer, tn=512):
    m, d = x2d.shape
    dc = cfg.d_conv
    per = rows_per_batch // tm
    hx_blk, gb_blk, gc_blk = (cfg.off[n] // dc for n in ("hx", "gb", "gc"))
    kern = functools.partial(_mix_out_kernel, tm=tm, seq_tiles=seq_tiles, d_conv=dc, d_diff=cfg.d_diff,
                             cols_outer=cols_outer)

    def rc(index_map):
        return (lambda a, b: index_map(b, a)) if cols_outer else index_map

    mix_shape = (m // tm, tm, d) if cols_outer else (1, tm, d)
    return pl.pallas_call(
        kern,
        grid=(d // tn, m // tm) if cols_outer else (m // tm, d // tn),
        in_specs=[pl.BlockSpec((tm, tn), rc(lambda i, j: (i, j))),
                  pl.BlockSpec((tm, dc), rc(lambda i, j: (i, hx_blk))),
                  pl.BlockSpec((tm, dc), rc(lambda i, j: (i, gb_blk))),
                  pl.BlockSpec((tm, dc), rc(lambda i, j: (i, gc_blk))),
                  pl.BlockSpec((SUBLANES, dc), rc(halo_map_a)),
                  pl.BlockSpec((SUBLANES, dc), rc(halo_map_b)),
                  pl.BlockSpec(conv_w.shape, rc(lambda i, j: (0, 0))),
                  pl.BlockSpec((tm, cfg.d_diff), rc(lambda i, j: (i, 0))),
                  pl.BlockSpec((tm, cfg.d_dsa), rc(lambda i, j: (i, 0))),
                  pl.BlockSpec((tm, d), rc(lambda i, j: (i, 0))),
                  pl.BlockSpec((None, 3, tn), rc(lambda i, j: (i // per, 0, j))),
                  pl.BlockSpec((None, d, tn), rc(lambda i, j: (layer, 0, j)))],
        out_specs=[pl.BlockSpec((tm, tn), rc(lambda i, j: (i, j))),
                   pl.BlockSpec((None, None, SUBLANES, dc), rc(lambda i, j: (j if cols_outer else 0, i, 0, 0)))],
        out_shape=[jax.ShapeDtypeStruct((m, d), F32),
                   jax.ShapeDtypeStruct((d // tn if cols_outer else 1, m // tm, SUBLANES, dc), F32)],
        scratch_shapes=[pltpu.VMEM(mix_shape, F32 if cols_outer else BF16)],
        compiler_params=_params(("arbitrary", "arbitrary"), 56),
        name="mix_out",
    )(x2d, p_x, p_x, p_x, halo_a, halo_b, conv_w, y_diff, y_dsa, z_x, mod3, wo16)


ATTN_PAGES = 16


def _page_specs(layer, pps, rows, width):
    def spec(p):
        return pl.BlockSpec((None, None, rows, width),
                            lambda b, s, pt: (layer, pt[b, s * pps + p], 0, 0))
    return [spec(p) for p in range(pps)]


def _cat_pages(refs, g, n_kv):
    return jnp.concatenate([r[pl.ds(g, PAGE_SIZE, stride=n_kv), :] for r in refs], axis=0).astype(BF16)


def _dec_diff_kernel(pt_ref, q_ref, *refs, pps, n_kv, rep, t_new, lam_init):
    kp, vp = refs[:pps], refs[pps:2 * pps]
    knew_ref, vnew_ref, lam_ref, sg_ref, o_ref, qs_sc, m_sc, l_sc, acc_sc = refs[2 * pps:]
    step = pl.program_id(1)
    rg = 2 * rep * SUBLANES
    groups = [slice(g * rg, (g + 1) * rg) for g in range(n_kv)]

    @pl.when(step == 0)
    def _():
        for h in range(n_kv * rep):
            qh = q_ref[:, h * HEAD_DIM:(h + 1) * HEAD_DIM] * ((HEAD_DIM // 2) ** -0.5)
            lo, hi = _half_masked(qh, SUBLANES)
            qs_sc[2 * h * SUBLANES:(2 * h + 1) * SUBLANES] = lo
            qs_sc[(2 * h + 1) * SUBLANES:(2 * h + 2) * SUBLANES] = hi
        _init_flash(m_sc, l_sc, acc_sc)

    scores = [_dot_nt(qs_sc[rows].astype(BF16), _cat_pages(kp, g, n_kv)) for g, rows in enumerate(groups)]
    for g, rows in enumerate(groups):
        _online_update(m_sc, l_sc, acc_sc, rows, scores[g], _cat_pages(vp, g, n_kv), False)

    @pl.when(step == pl.num_programs(1) - 1)
    def _():
        tok = lax.broadcasted_iota(jnp.int32, (rg, LANES), 0) % SUBLANES
        col = lax.broadcasted_iota(jnp.int32, (rg, LANES), 1)
        valid = (col <= tok) & (col < t_new)
        for g, rows in enumerate(groups):
            c0, c1 = g * HEAD_DIM, (g + 1) * HEAD_DIM
            s = jnp.where(valid, _dot_nt(qs_sc[rows].astype(BF16), knew_ref[:, c0:c1]), NEG_INF)
            _online_update(m_sc, l_sc, acc_sc, rows, s, vnew_ref[:, c0:c1], False)
        lmb = _lambda_full(lam_ref, lam_init)
        o_all = acc_sc[...] / l_sc[...]
        for h in range(n_kv * rep):
            o0 = o_all[2 * h * SUBLANES:(2 * h + 1) * SUBLANES]
            o1 = o_all[(2 * h + 1) * SUBLANES:(2 * h + 2) * SUBLANES]
            o_ref[:, h * HEAD_DIM:(h + 1) * HEAD_DIM] = _diff_head_out(o0, o1, lmb, sg_ref[...], lam_init)


def _flash_scratch(rows):
    return [pltpu.VMEM((rows, HEAD_DIM), F32), pltpu.VMEM((rows, LANES), F32),
            pltpu.VMEM((rows, LANES), F32), pltpu.VMEM((rows, HEAD_DIM), F32)]


def _dec_diff_call(page_table, p_s, cache_k, cache_v, knew16, vnew16, lam, sg, cfg, layer, t_new, lam_init):
    nb, n_pages = page_table.shape
    pps = math.gcd(n_pages, ATTN_PAGES)
    kvw = cfg.n_kv_diff * HEAD_DIM
    rows = cfg.n_kv_diff * 2 * cfg.rep_diff * SUBLANES
    q_blk = cfg.off["q_d"] // cfg.d_diff
    kern = functools.partial(_dec_diff_kernel, pps=pps, n_kv=cfg.n_kv_diff, rep=cfg.rep_diff,
                             t_new=t_new, lam_init=lam_init)
    grid_spec = pltpu.PrefetchScalarGridSpec(
        num_scalar_prefetch=1,
        grid=(nb, n_pages // pps),
        in_specs=[pl.BlockSpec((SUBLANES, cfg.d_diff), lambda b, s, pt: (b, q_blk))]
        + 2 * _page_specs(layer, pps, PAGE_SIZE * cfg.n_kv_diff, HEAD_DIM)
        + [pl.BlockSpec((None, PAGE_SIZE, kvw), lambda b, s, pt: (b, 0, 0)),
           pl.BlockSpec((None, PAGE_SIZE, kvw), lambda b, s, pt: (b, 0, 0)),
           pl.BlockSpec(lam.shape, lambda b, s, pt: (0, 0)),
           pl.BlockSpec((1, HEAD_DIM), lambda b, s, pt: (0, 0))],
        out_specs=pl.BlockSpec((SUBLANES, cfg.d_diff), lambda b, s, pt: (b, 0)),
        scratch_shapes=_flash_scratch(rows))
    return pl.pallas_call(
        kern,
        grid_spec=grid_spec,
        out_shape=jax.ShapeDtypeStruct((nb * SUBLANES, cfg.d_diff), F32),
        compiler_params=_params(("arbitrary", "arbitrary"), 48),
        name="diff_attn_sample",
    )(page_table, p_s, *([cache_k] * pps), *([cache_v] * pps), knew16, vnew16, lam, sg.reshape(1, HEAD_DIM))


def _dec_index_kernel(pt_ref, q_ref, w_ref, *refs, pps, k_top, t_new, past_len):
    kp = refs[:pps]
    knew_ref, bias_ref, sc_sc = refs[pps:]
    step = pl.program_id(1)
    span = pps * PAGE_SIZE

    def idx_score(s):
        s = jnp.maximum(s, 0.0) * (w_ref[...] * (N_IDX_HEADS ** -0.5))
        return jnp.sum(s.reshape(N_IDX_HEADS, SUBLANES, s.shape[-1]), axis=0)

    k_t = jnp.concatenate([r[...] for r in kp], axis=1).astype(BF16)
    off = pl.multiple_of(step * span, span)
    sc_sc[:, pl.ds(off, span)] = idx_score(jnp.dot(q_ref[...], k_t, preferred_element_type=F32))

    @pl.when(step == pl.num_programs(1) - 1)
    def _():
        tok = lax.broadcasted_iota(jnp.int32, (SUBLANES, LANES), 0)
        col = lax.broadcasted_iota(jnp.int32, (SUBLANES, LANES), 1)
        valid = (col <= tok) & (col < t_new)
        sc_sc[:, past_len:past_len + LANES] = jnp.where(valid, idx_score(_dot_nt(q_ref[...], knew_ref[...])), NEG_INF)
        bias_ref[...] = _selection_bias(sc_sc[...], k_top)


def _dec_index_call(page_table, qi_rows, wi_col, cache_kidx_t, kinew16, layer, t_new):
    nb, n_pages = page_table.shape
    pps = math.gcd(n_pages, ATTN_PAGES)
    past_len = n_pages * PAGE_SIZE
    k_top = min(TOPK_MAX, (past_len + t_new) // 4)
    w = past_len + LANES
    rows = N_IDX_HEADS * SUBLANES
    kern = functools.partial(_dec_index_kernel, pps=pps, k_top=k_top, t_new=t_new, past_len=past_len)
    grid_spec = pltpu.PrefetchScalarGridSpec(
        num_scalar_prefetch=1,
        grid=(nb, n_pages // pps),
        in_specs=[pl.BlockSpec((None, rows, D_IDX), lambda b, s, pt: (b, 0, 0)),
                  pl.BlockSpec((None, rows, 1), lambda b, s, pt: (b, 0, 0))]
        + _page_specs(layer, pps, D_IDX, PAGE_SIZE)
        + [pl.BlockSpec((None, PAGE_SIZE, D_IDX), lambda b, s, pt: (b, 0, 0))],
        out_specs=pl.BlockSpec((None, SUBLANES, w), lambda b, s, pt: (b, 0, 0)),
        scratch_shapes=[pltpu.VMEM((SUBLANES, w), F32)])
    return pl.pallas_call(
        kern,
        grid_spec=grid_spec,
        out_shape=jax.ShapeDtypeStruct((nb, SUBLANES, w), F32),
        compiler_params=_params(("arbitrary", "arbitrary"), 40),
        name="dsa_index_sample",
    )(page_table, qi_rows, wi_col, *([cache_kidx_t] * pps), kinew16)


def _dec_dsa_kernel(pt_ref, q_ref, bias_ref, bnew_ref, *refs, pps, n_kv, rep):
    kp, vp = refs[:pps], refs[pps:2 * pps]
    knew_ref, vnew_ref, o_ref, qs_sc, m_sc, l_sc, acc_sc = refs[2 * pps:]
    step = pl.program_id(1)
    rg = rep * SUBLANES
    groups = [slice(g * rg, (g + 1) * rg) for g in range(n_kv)]
    scale = HEAD_DIM ** -0.5

    @pl.when(step == 0)
    def _():
        for h in range(n_kv * rep):
            qs_sc[h * SUBLANES:(h + 1) * SUBLANES] = q_ref[:, h * HEAD_DIM:(h + 1) * HEAD_DIM]
        _init_flash(m_sc, l_sc, acc_sc)

    def scores(g, k16, bias):
        s = _dot_nt(qs_sc[groups[g]].astype(BF16), k16) * scale
        n = s.shape[-1]
        return (s.reshape(rep, SUBLANES, n) + bias[None]).reshape(rg, n)

    s_all = [scores(g, _cat_pages(kp, g, n_kv), bias_ref[...]) for g in range(n_kv)]
    for g in range(n_kv):
        _online_update(m_sc, l_sc, acc_sc, groups[g], s_all[g], _cat_pages(vp, g, n_kv), True)

    @pl.when(step == pl.num_programs(1) - 1)
    def _():
        for g in range(n_kv):
            c0, c1 = g * HEAD_DIM, (g + 1) * HEAD_DIM
            _online_update(m_sc, l_sc, acc_sc, groups[g], scores(g, knew_ref[:, c0:c1], bnew_ref[...]),
                           vnew_ref[:, c0:c1], True)
        o_all = acc_sc[...] / l_sc[...]
        for h in range(n_kv * rep):
            o_ref[:, h * HEAD_DIM:(h + 1) * HEAD_DIM] = o_all[h * SUBLANES:(h + 1) * SUBLANES]


def _dec_dsa_call(page_table, p_s, bias, cache_k, cache_v, knew16, vnew16, cfg, layer):
    nb, n_pages = page_table.shape
    pps = math.gcd(n_pages, ATTN_PAGES)
    kvw = cfg.n_kv_dsa * HEAD_DIM
    rows = cfg.n_kv_dsa * cfg.rep_dsa * SUBLANES
    span = pps * PAGE_SIZE
    new_blk = n_pages * PAGE_SIZE // LANES
    q_blk = cfg.off["q_s"] // cfg.d_dsa
    kern = functools.partial(_dec_dsa_kernel, pps=pps, n_kv=cfg.n_kv_dsa, rep=cfg.rep_dsa)
    grid_spec = pltpu.PrefetchScalarGridSpec(
        num_scalar_prefetch=1,
        grid=(nb, n_pages // pps),
        in_specs=[pl.BlockSpec((SUBLANES, cfg.d_dsa), lambda b, s, pt: (b, q_blk)),
                  pl.BlockSpec((None, SUBLANES, span), lambda b, s, pt: (b, 0, s)),
                  pl.BlockSpec((None, SUBLANES, LANES), lambda b, s, pt: (b, 0, new_blk))]
        + 2 * _page_specs(layer, pps, PAGE_SIZE * cfg.n_kv_dsa, HEAD_DIM)
        + [pl.BlockSpec((None, PAGE_SIZE, kvw), lambda b, s, pt: (b, 0, 0)),
           pl.BlockSpec((None, PAGE_SIZE, kvw), lambda b, s, pt: (b, 0, 0))],
        out_specs=pl.BlockSpec((SUBLANES, cfg.d_dsa), lambda b, s, pt: (b, 0)),
        scratch_shapes=_flash_scratch(rows))
    return pl.pallas_call(
        kern,
        grid_spec=grid_spec,
        out_shape=jax.ShapeDtypeStruct((nb * SUBLANES, cfg.d_dsa), F32),
        compiler_params=_params(("arbitrary", "arbitrary"), 40),
        name="dsa_attn_sample",
    )(page_table, p_s, bias, bias, *([cache_k] * pps), *([cache_v] * pps), knew16, vnew16)


def _pad_rows(a, rows):
    return jnp.pad(a, ((0, 0), (0, rows - a.shape[1]), (0, 0)))


def kernel(x_prompt, x_sample, cache_diff_k, cache_diff_v, cache_dsa_k, cache_dsa_v, cache_dsa_kidx,
           state_conv, page_table, c_prompt, c_sample, norm_g, w_ada, b_ada, w_in, conv_w,
           diff_lam, diff_subln_g, w_out, final_g):
    nbp, t, d = x_prompt.shape
    nbs, t_new, _ = x_sample.shape
    depth, n_pool = cache_diff_k.shape[:2]
    cfg = _make_cfg(d, cache_diff_k.shape[3], cache_dsa_k.shape[3])
    dc = cfg.d_conv
    off = cfg.off
    kdw, ksw = cfg.n_kv_diff * HEAD_DIM, cfg.n_kv_dsa * HEAD_DIM
    kiwi_w = D_IDX + N_IDX_HEADS
    assert cache_diff_k.shape[2] == PAGE_SIZE and cache_diff_k.shape[4] == HEAD_DIM
    assert state_conv.shape[2] == CONV_WIDTH - 1 and CONV_WIDTH - 1 <= t_new <= SUBLANES
    assert w_in.shape[2] == cfg.n_main + kiwi_w + d and cfg.d_diff == cfg.d_dsa

    w_main16 = w_in.astype(BF16)
    w_kiwi16 = jnp.pad(w_in[:, :, cfg.n_main:cfg.n_main + kiwi_w], ((0, 0), (0, 0), (0, LANES - kiwi_w))).astype(BF16)
    w_z16 = w_in[:, :, cfg.n_main + kiwi_w:].astype(BF16)
    wo16 = w_out.astype(BF16)
    z_perm = tuple(range(d // PROJ_TN))

    n_c = nbp + nbs
    c_all = _pad_rows(jnp.concatenate([c_prompt, c_sample], axis=0)[None], -(-n_c // SUBLANES) * SUBLANES)[0]
    mod = _mod_call(c_all, w_ada, b_ada).reshape(depth, c_all.shape[0], 3, d)

    ck_d = cache_diff_k.reshape(depth, n_pool, PAGE_SIZE * cfg.n_kv_diff, HEAD_DIM)
    cv_d = cache_diff_v.reshape(depth, n_pool, PAGE_SIZE * cfg.n_kv_diff, HEAD_DIM)
    ck_s = cache_dsa_k.reshape(depth, n_pool, PAGE_SIZE * cfg.n_kv_dsa, HEAD_DIM)
    cv_s = cache_dsa_v.reshape(depth, n_pool, PAGE_SIZE * cfg.n_kv_dsa, HEAD_DIM)
    ck_i = jnp.swapaxes(cache_dsa_kidx, 2, 3)

    xp = x_prompt.reshape(nbp * t, d)
    xs = _pad_rows(x_sample, SUBLANES).reshape(nbs * SUBLANES, d)
    ones_halo = jnp.ones((SUBLANES, dc), F32)
    tm_p = 512
    seq_tiles = t // tm_p
    halo_blocks = tm_p // SUBLANES
    ms = nbs * SUBLANES

    def prompt_halo(name):
        col_blk = off[name] // dc
        return lambda i, j: (jnp.maximum(i * halo_blocks - 1, 0), col_blk)

    def project(h16, l, tm, z_dtype):
        p = _matmul_call(h16, w_main16, l, cfg.perm, F32, tm, PROJ_TN)
        z = _matmul_call(h16, w_z16, l, z_perm, z_dtype, tm, PROJ_TN)
        kiwi = _matmul_call(h16, w_kiwi16, l, (0,), F32, tm, LANES)
        return p, z, kiwi

    p_rows, s_rows = [], []
    for l in range(depth):
        lam_init = LAMBDA_INIT_BASE - LAMBDA_INIT_AMP * math.exp(-LAMBDA_INIT_RATE * l)
        mod_p, mod_s = mod[l, :nbp], mod[l, nbp:n_c]

        h_p = _norm_mod_call(xp, norm_g[l], mod_p, t, 256, BF16)
        p_p, z_p, kiwi_p = project(h_p, l, 1024, BF16)
        yd_p = _diff_prompt_call(p_p, diff_lam[l], diff_subln_g[l], cfg, nbp, t, lam_init)
        ys_p = _dsa_prompt_call(p_p, kiwi_p, cfg, nbp, t)
        xp, ut_p = _mix_out_call(xp, p_p, p_p, p_p, prompt_halo("hx"), prompt_halo("gc"), conv_w[l], yd_p, ys_p,
                                 z_p, mod_p, wo16, l, cfg, t, tm_p, seq_tiles, False)
        p_rows.append((p_p, kiwi_p,
                       ut_p.reshape(nbp, seq_tiles, SUBLANES, dc)[:, -1, SUBLANES - (CONV_WIDTH - 1):]))

        h_s = _norm_mod_call(xs, norm_g[l], mod_s, SUBLANES, SUBLANES, F32).astype(BF16)
        p_s, z_s, kiwi_s = project(h_s, l, ms, F32)
        p_s3 = p_s.reshape(nbs, SUBLANES, cfg.n_main)
        kiwi_s3 = kiwi_s.reshape(nbs, SUBLANES, LANES)

        def new_rows(name, width):
            return _pad_rows(p_s3[:, :, off[name]:off[name] + width], PAGE_SIZE).astype(BF16)

        yd_s = _dec_diff_call(page_table, p_s, ck_d, cv_d, new_rows("k_d", kdw), new_rows("v_d", kdw),
                              diff_lam[l], diff_subln_g[l], cfg, l, t_new, lam_init)
        qi_rows = p_s3[:, :, off["qi"]:off["qi"] + N_IDX_HEADS * D_IDX].reshape(nbs, SUBLANES, N_IDX_HEADS, D_IDX)
        qi_rows = qi_rows.transpose(0, 2, 1, 3).reshape(nbs, N_IDX_HEADS * SUBLANES, D_IDX).astype(BF16)
        wi_col = kiwi_s3[:, :, D_IDX:kiwi_w].transpose(0, 2, 1).reshape(nbs, N_IDX_HEADS * SUBLANES, 1)
        ki_new = _pad_rows(kiwi_s3[:, :, :D_IDX], PAGE_SIZE).astype(BF16)
        bias = _dec_index_call(page_table, qi_rows, wi_col, ck_i, ki_new, l, t_new)
        ys_s = _dec_dsa_call(page_table, p_s, bias, ck_s, cv_s, new_rows("k_s", ksw), new_rows("v_s", ksw),
                             cfg, l)
        state_pad = jnp.pad(state_conv[l], ((0, 0), (SUBLANES - (CONV_WIDTH - 1), 0), (0, 0)))
        xs, ut_s = _mix_out_call(xs, p_s, state_pad.reshape(ms, dc), ones_halo,
                                 lambda i, j: (i, 0), lambda i, j: (0, 0), conv_w[l], yd_s, ys_s,
                                 z_s, mod_s, wo16, l, cfg, SUBLANES, SUBLANES, 0, True)
        s_rows.append((p_s3[:, :t_new], kiwi_s3[:, :t_new], ut_s[0, :, t_new - (CONV_WIDTH - 1):t_new]))

    y_prompt = _rmsnorm_call(xp, final_g, 256).reshape(nbp, t, d)
    y_sample = _rmsnorm_call(xs, final_g, SUBLANES).reshape(nbs, SUBLANES, d)[:, :t_new]

    def stack_p(name, n_kv):
        return jnp.stack([r[0][:, off[name]:off[name] + n_kv * HEAD_DIM].reshape(nbp, t, n_kv, HEAD_DIM)
                          for r in p_rows])

    def stack_s(name, n_kv):
        return jnp.stack([r[0][:, :, off[name]:off[name] + n_kv * HEAD_DIM].reshape(nbs, t_new, n_kv, HEAD_DIM)
                          for r in s_rows])

    return (y_prompt, y_sample,
            stack_p("k_d", cfg.n_kv_diff), stack_p("v_d", cfg.n_kv_diff),
            stack_p("k_s", cfg.n_kv_dsa), stack_p("v_s", cfg.n_kv_dsa),
            jnp.stack([r[1][:, :D_IDX].reshape(nbp, t, D_IDX) for r in p_rows]),
            jnp.stack([r[2] for r in p_rows]),
            stack_s("k_d", cfg.n_kv_diff), stack_s("v_d", cfg.n_kv_diff),
            stack_s("k_s", cfg.n_kv_dsa), stack_s("v_s", cfg.n_kv_dsa),
            jnp.stack([r[1][:, :, :D_IDX] for r in s_rows]),
            jnp.stack([r[2] for r in s_rows]))
```

```python
import functools
import math
from typing import NamedTuple

import jax
import jax.numpy as jnp
from jax import lax
from jax.experimental import pallas as pl
from jax.experimental.pallas import tpu as pltpu

F32 = jnp.float32
BF16 = jnp.bfloat16

HEAD_DIM = 128
D_IDX = 64
N_IDX_HEADS = 16
CONV_WIDTH = 3
TOPK_MAX = 256
PAGE_SIZE = 128
RMS_EPS = 1e-6
LAMBDA_INIT_BASE = 0.8
LAMBDA_INIT_AMP = 0.6
LAMBDA_INIT_RATE = 0.3

LANES = 128
SUBLANES = 8
MIB = 2 ** 20
PROJ_TN = 512

NEG_INF = float("-inf")
INT_MIN = -(2 ** 31)
KEY_NEG_INF = INT_MIN + 0x7FFFFF


class Cfg(NamedTuple):
    d_model: int
    d_conv: int
    n_kv_diff: int
    rep_diff: int
    d_diff: int
    n_kv_dsa: int
    rep_dsa: int
    d_dsa: int
    n_main: int
    off: dict
    perm: tuple


def _make_cfg(d_model, n_kv_diff, n_kv_dsa):
    d_conv = d_model // 4
    d_diff = 3 * d_model // 8
    d_dsa = 3 * d_model // 8
    kd, ks = n_kv_diff * HEAD_DIM, n_kv_dsa * HEAD_DIM
    qi = N_IDX_HEADS * D_IDX
    w_order = (("hx", d_conv), ("gb", d_conv), ("gc", d_conv), ("q_d", d_diff), ("k_d", kd), ("v_d", kd),
               ("q_s", d_dsa), ("kv_s", 2 * ks), ("qi", qi))
    out_order = ("q_d", "q_s", "qi", "hx", "gb", "gc", "k_d", "v_d", "kv_s")
    width = dict(w_order)
    off, pos = {}, 0
    for name in out_order:
        assert width[name] % PROJ_TN == 0
        off[name] = pos
        pos += width[name]
    off["k_s"], off["v_s"] = off["kv_s"], off["kv_s"] + ks
    perm = []
    for name, wd in w_order:
        perm.extend(off[name] // PROJ_TN + b for b in range(wd // PROJ_TN))
    return Cfg(d_model, d_conv, n_kv_diff, d_diff // HEAD_DIM // n_kv_diff, d_diff,
               n_kv_dsa, d_dsa // HEAD_DIM // n_kv_dsa, d_dsa, pos, off, tuple(perm))


def _params(semantics, vmem_mib):
    return pltpu.CompilerParams(dimension_semantics=semantics, vmem_limit_bytes=vmem_mib * MIB)


def _dot_nt(a, b):
    return lax.dot_general(a, b, (((1,), (1,)), ((), ())), preferred_element_type=F32)


MOD_STREAMS = 8
MOD_ROWS = 32


def _mod_kernel(c_ref, *refs):
    w_refs, (b_ref, o_ref) = refs[:MOD_STREAMS], refs[MOD_STREAMS:]

    @pl.when(pl.program_id(1) == 0)
    def _():
        o_ref[...] = jnp.broadcast_to(b_ref[...], o_ref.shape)

    c = c_ref[...]
    a = (c * jax.nn.sigmoid(c)).astype(BF16)
    w = jnp.concatenate([r[...].astype(BF16) for r in w_refs], axis=0)
    o_ref[...] += jnp.dot(a, w, preferred_element_type=F32)


def _mod_call(c_all, w_ada, b_ada):
    depth, d, n3 = w_ada.shape
    rows = c_all.shape[0]
    tk = MOD_STREAMS * MOD_ROWS

    def w_spec(p):
        return pl.BlockSpec((None, MOD_ROWS, n3), lambda l, k: (l, k * MOD_STREAMS + p, 0))

    return pl.pallas_call(
        _mod_kernel,
        grid=(depth, d // tk),
        in_specs=[pl.BlockSpec((rows, tk), lambda l, k: (0, k))]
        + [w_spec(p) for p in range(MOD_STREAMS)]
        + [pl.BlockSpec((None, 1, n3), lambda l, k: (l, 0, 0))],
        out_specs=pl.BlockSpec((None, rows, n3), lambda l, k: (l, 0, 0)),
        out_shape=jax.ShapeDtypeStruct((depth, rows, n3), F32),
        compiler_params=_params(("arbitrary", "arbitrary"), 48),
        name="adaln_mod",
    )(c_all, *([w_ada] * MOD_STREAMS), b_ada.reshape(depth, 1, n3))


def _rms(x):
    return x * lax.rsqrt(jnp.mean(x * x, axis=-1, keepdims=True) + RMS_EPS)


NORM_STREAMS = 4


def _row_streams(tm):
    return NORM_STREAMS if tm % (NORM_STREAMS * SUBLANES) == 0 else 1


def _row_stream_specs(tm, d, streams):
    rows = tm // streams

    def spec(p):
        return pl.BlockSpec((rows, d), lambda i: (i * streams + p, 0))
    return [spec(p) for p in range(streams)]


def _norm_mod_kernel(*refs):
    x_refs, (g_ref, mod_ref, o_ref) = refs[:-3], refs[-3:]
    x = jnp.concatenate([r[...] for r in x_refs], axis=0)
    y = _rms(x) * g_ref[...]
    o_ref[...] = (y * (1.0 + mod_ref[1:2, :]) + mod_ref[0:1, :]).astype(o_ref.dtype)


def _norm_mod_call(x2d, g, mod3, rows_per_batch, tm, out_dtype):
    m, d = x2d.shape
    per = rows_per_batch // tm
    streams = _row_streams(tm)
    return pl.pallas_call(
        _norm_mod_kernel,
        grid=(m // tm,),
        in_specs=_row_stream_specs(tm, d, streams)
        + [pl.BlockSpec((1, d), lambda i: (0, 0)),
           pl.BlockSpec((None, 3, d), lambda i: (i // per, 0, 0))],
        out_specs=pl.BlockSpec((tm, d), lambda i: (i, 0)),
        out_shape=jax.ShapeDtypeStruct((m, d), out_dtype),
        compiler_params=_params(("arbitrary",), 40),
        name="norm_mod",
    )(*([x2d] * streams), g.reshape(1, d), mod3)


def _rmsnorm_kernel(*refs):
    x_refs, (g_ref, o_ref) = refs[:-2], refs[-2:]
    o_ref[...] = _rms(jnp.concatenate([r[...] for r in x_refs], axis=0)) * g_ref[...]


def _rmsnorm_call(x2d, g, tm):
    m, d = x2d.shape
    streams = _row_streams(tm)
    return pl.pallas_call(
        _rmsnorm_kernel,
        grid=(m // tm,),
        in_specs=_row_stream_specs(tm, d, streams) + [pl.BlockSpec((1, d), lambda i: (0, 0))],
        out_specs=pl.BlockSpec((tm, d), lambda i: (i, 0)),
        out_shape=jax.ShapeDtypeStruct((m, d), F32),
        compiler_params=_params(("arbitrary",), 40),
        name="final_norm",
    )(*([x2d] * streams), g.reshape(1, d))


def _matmul_kernel(perm_ref, x_ref, w_ref, o_ref):
    o_ref[...] = jnp.dot(x_ref[...], w_ref[...], preferred_element_type=F32).astype(o_ref.dtype)


def _matmul_call(x16, w16, layer, perm, out_dtype, tm, tn):
    m, k = x16.shape
    n_blocks = len(perm)
    grid_spec = pltpu.PrefetchScalarGridSpec(
        num_scalar_prefetch=1,
        grid=(m // tm, n_blocks),
        in_specs=[pl.BlockSpec((tm, k), lambda i, j, pm: (i, 0)),
                  pl.BlockSpec((None, k, tn), lambda i, j, pm: (layer, 0, j))],
        out_specs=pl.BlockSpec((tm, tn), lambda i, j, pm: (i, pm[j])))
    return pl.pallas_call(
        _matmul_kernel,
        grid_spec=grid_spec,
        out_shape=jax.ShapeDtypeStruct((m, n_blocks * tn), out_dtype),
        compiler_params=_params(("arbitrary", "arbitrary"), 48),
        name="in_proj",
    )(jnp.asarray(perm, jnp.int32), x16, w16)


def _half_masked(q, rows):
    lane = lax.broadcasted_iota(jnp.int32, (rows, HEAD_DIM), 1)
    zero = jnp.zeros_like(q)
    return jnp.where(lane < HEAD_DIM // 2, q, zero), jnp.where(lane >= HEAD_DIM // 2, q, zero)


def _lambda_full(lam_ref, lam_init):
    lam = lam_ref[...]
    a = jnp.sum(lam[0:1, :] * lam[1:2, :], axis=-1, keepdims=True)
    b = jnp.sum(lam[2:3, :] * lam[3:4, :], axis=-1, keepdims=True)
    return jnp.exp(a) - jnp.exp(b) + lam_init


def _diff_head_out(o0, o1, lmb, sg, lam_init):
    o = o0 - lmb * o1
    return _rms(o) * sg * (1.0 - lam_init)


def _online_update(m_sc, l_sc, acc_sc, rows, s, v16, guard_empty):
    m_prev = m_sc[rows]
    m_new = jnp.maximum(m_prev, jnp.max(s, axis=-1, keepdims=True))
    m_use = jnp.where(m_new == NEG_INF, 0.0, m_new) if guard_empty else m_new
    alpha = jnp.exp(m_prev - m_use)
    p = jnp.exp(s - jnp.concatenate([m_use] * (s.shape[-1] // LANES), axis=1))
    l_sc[rows] = alpha * l_sc[rows] + jnp.sum(p, axis=-1, keepdims=True)
    acc_sc[rows] = alpha * acc_sc[rows] + jnp.dot(p.astype(BF16), v16, preferred_element_type=F32)
    m_sc[rows] = m_new


def _init_flash(m_sc, l_sc, acc_sc):
    m_sc[...] = jnp.full_like(m_sc, NEG_INF)
    l_sc[...] = jnp.zeros_like(l_sc)
    acc_sc[...] = jnp.zeros_like(acc_sc)


def _diff_prompt_kernel(q_ref, k_ref, v_ref, lam_ref, sg_ref, o_ref, qs_sc, m_sc, l_sc, acc_sc,
                        *, rep, tq, lam_init):
    i = pl.program_id(2)
    chunk = LANES
    n_chunks = 2 * rep * tq // chunk
    for r in range(rep):
        qr = q_ref[:, r * HEAD_DIM:(r + 1) * HEAD_DIM] * ((HEAD_DIM // 2) ** -0.5)
        lo, hi = _half_masked(qr, tq)
        qs_sc[(2 * r) * tq:(2 * r + 1) * tq] = lo.astype(BF16)
        qs_sc[(2 * r + 1) * tq:(2 * r + 2) * tq] = hi.astype(BF16)
    _init_flash(m_sc, l_sc, acc_sc)

    def kv_block(j, diagonal):
        start = pl.multiple_of(j * tq, tq)
        k = k_ref[pl.ds(start, tq), :].astype(BF16)
        v = v_ref[pl.ds(start, tq), :].astype(BF16)
        for c in range(n_chunks):
            rows = slice(c * chunk, (c + 1) * chunk)
            s = _dot_nt(qs_sc[rows], k)
            if diagonal:
                q0 = (c * chunk) % tq
                row = lax.broadcasted_iota(jnp.int32, (chunk, tq), 0) + q0
                col = lax.broadcasted_iota(jnp.int32, (chunk, tq), 1)
                s = jnp.where(col <= row, s, NEG_INF)
            _online_update(m_sc, l_sc, acc_sc, rows, s, v, False)

    def full_block(j, carry):
        kv_block(j, False)
        return carry

    lax.fori_loop(0, i, full_block, 0)
    kv_block(i, True)

    lmb = _lambda_full(lam_ref, lam_init)
    for r in range(rep):
        lo = slice((2 * r) * tq, (2 * r + 1) * tq)
        hi = slice((2 * r + 1) * tq, (2 * r + 2) * tq)
        y = _diff_head_out(acc_sc[lo] / l_sc[lo], acc_sc[hi] / l_sc[hi], lmb, sg_ref[...], lam_init)
        o_ref[:, r * HEAD_DIM:(r + 1) * HEAD_DIM] = y.astype(BF16)


def _diff_prompt_call(p_p, lam, sg, cfg, nb, t, lam_init, tq=256):
    m = p_p.shape[0]
    nq = t // tq
    g_w = cfg.rep_diff * HEAD_DIM
    rows = 2 * cfg.rep_diff * tq
    q_blk, k_blk, v_blk = cfg.off["q_d"] // g_w, cfg.off["k_d"] // HEAD_DIM, cfg.off["v_d"] // HEAD_DIM
    kern = functools.partial(_diff_prompt_kernel, rep=cfg.rep_diff, tq=tq, lam_init=lam_init)
    return pl.pallas_call(
        kern,
        grid=(nb, cfg.n_kv_diff, nq),
        in_specs=[pl.BlockSpec((tq, g_w), lambda b, g, i: (b * nq + i, q_blk + g)),
                  pl.BlockSpec((t, HEAD_DIM), lambda b, g, i: (b, k_blk + g)),
                  pl.BlockSpec((t, HEAD_DIM), lambda b, g, i: (b, v_blk + g)),
                  pl.BlockSpec(lam.shape, lambda b, g, i: (0, 0)),
                  pl.BlockSpec((1, HEAD_DIM), lambda b, g, i: (0, 0))],
        out_specs=pl.BlockSpec((tq, g_w), lambda b, g, i: (b * nq + i, g)),
        out_shape=jax.ShapeDtypeStruct((m, cfg.d_diff), BF16),
        scratch_shapes=[pltpu.VMEM((rows, HEAD_DIM), BF16), pltpu.VMEM((rows, LANES), F32),
                        pltpu.VMEM((rows, LANES), F32), pltpu.VMEM((rows, HEAD_DIM), F32)],
        compiler_params=_params(("arbitrary", "arbitrary", "arbitrary"), 40),
        name="diff_attn_prompt",
    )(p_p, p_p, p_p, lam, sg.reshape(1, HEAD_DIM))


def _monotone_key(x):
    bits = lax.bitcast_convert_type(x, jnp.int32)
    key = bits ^ ((bits >> 31) & jnp.int32(0x7FFFFFFF))
    return jnp.where(bits == jnp.int32(INT_MIN), 0, key)


def _count(mask):
    return jnp.sum(jnp.where(mask, 1.0, 0.0), axis=-1, keepdims=True)


def _topk_mask(score, k_top):
    r, w = score.shape
    key = _monotone_key(score)
    kf = float(k_top)

    def value_bit(it, t):
        cand = t + (jnp.int32(1) << (31 - it))
        return jnp.where(_count(key >= cand) >= kf, cand, t)

    t = lax.fori_loop(0, 32, value_bit, jnp.full((r, 1), INT_MIN, jnp.int32))
    gt = key > t
    eq = key == t
    n_gt = _count(gt)
    need = kf - n_gt
    col = lax.broadcasted_iota(jnp.int32, (r, w), 1)
    ambiguous = (n_gt + _count(eq) > kf) & (t > KEY_NEG_INF)
    n_bits = w.bit_length()

    def tie_search(_):
        def index_bit(it, j):
            cand = j + (jnp.int32(1) << (n_bits - 1 - it))
            return jnp.where(_count(eq & (col < cand)) < need, cand, j)
        return lax.fori_loop(0, n_bits, index_bit, jnp.zeros((r, 1), jnp.int32))

    any_amb = jnp.max(jnp.where(ambiguous, 1.0, 0.0)) > 0.5
    j = lax.cond(any_amb, tie_search, lambda _: jnp.full((r, 1), w, jnp.int32), 0)
    return gt | (eq & (col <= j))


def _selection_bias(score, k_top):
    sel = _topk_mask(score, k_top) & (score > NEG_INF)
    return jnp.where(sel, 0.0, NEG_INF)


DSA_UNIT = 128
DSA_UNITS = 2


def _dsa_prompt_kernel(qi_ref, wi_ref, qs_ref, ki_ref, ks_ref, vs_ref, o_ref, kiki_sc, ks_sc, vs_sc,
                       score_sc, bias_sc, *, widths, steps_per_width, k_top, n_kv, rep):
    i = pl.program_id(1)

    @pl.when(i == 0)
    def _():
        ki = ki_ref[...]
        lane = lax.broadcasted_iota(jnp.int32, ki.shape, 1)
        kiki_sc[...] = jnp.where(lane < D_IDX, ki, pltpu.roll(ki, D_IDX, 1)).astype(BF16)
        ks_sc[...] = ks_ref[...].astype(BF16)
        vs_sc[...] = vs_ref[...].astype(BF16)

    for c, w in enumerate(widths):
        pl.when(i // steps_per_width == c)(
            functools.partial(_dsa_prompt_body, qi_ref, wi_ref, qs_ref, kiki_sc, ks_sc, vs_sc, o_ref, score_sc,
                              bias_sc, i=i, w=w, k_top=k_top, n_kv=n_kv, rep=rep))


def _dsa_prompt_body(qi_ref, wi_ref, qs_ref, kiki_sc, ks_sc, vs_sc, o_ref, score_sc, bias_sc,
                     *, i, w, k_top, n_kv, rep):
    tq = DSA_UNIT
    first_row = i * (DSA_UNITS * tq)
    kiki = kiki_sc[0:w, :]

    def index_pass(u, carry):
        r0 = pl.multiple_of(u * tq, tq)
        rows = pl.ds(r0, tq)
        wi = wi_ref[rows, :][:, D_IDX:D_IDX + N_IDX_HEADS] * (N_IDX_HEADS ** -0.5)
        idx = jnp.zeros((tq, w), F32)
        for j in range(N_IDX_HEADS // 2):
            lo, hi = _half_masked(qi_ref[rows, j * LANES:(j + 1) * LANES], tq)
            sc = jnp.maximum(_dot_nt(jnp.concatenate([lo, hi], axis=0).astype(BF16), kiki), 0.0)
            idx = idx + sc[:tq] * wi[:, 2 * j:2 * j + 1] + sc[tq:] * wi[:, 2 * j + 1:2 * j + 2]
        row = lax.broadcasted_iota(jnp.int32, (tq, w), 0) + (first_row + r0)
        col = lax.broadcasted_iota(jnp.int32, (tq, w), 1)
        score_sc[rows, 0:w] = jnp.where(col <= row, idx, NEG_INF)
        return carry

    lax.fori_loop(0, DSA_UNITS, index_pass, 0)
    bias_sc[:, 0:w] = _selection_bias(score_sc[:, 0:w], k_top)

    def attend_pass(u, carry):
        rows = pl.ds(pl.multiple_of(u * tq, tq), tq)
        bias = bias_sc[rows, 0:w]
        for g in range(n_kv):
            heads = [qs_ref[rows, (g * rep + r) * HEAD_DIM:(g * rep + r + 1) * HEAD_DIM] for r in range(rep)]
            q = jnp.concatenate(heads, axis=0).astype(BF16)
            s = _dot_nt(q, ks_sc[0:w, g * HEAD_DIM:(g + 1) * HEAD_DIM]) * (HEAD_DIM ** -0.5)
            s = s.reshape(rep, tq, w) + bias[None]
            p = jnp.exp(s - jnp.max(s, axis=-1, keepdims=True))
            l = jnp.sum(p, axis=-1, keepdims=True).reshape(rep * tq, 1)
            o = jnp.dot(p.reshape(rep * tq, w).astype(BF16), vs_sc[0:w, g * HEAD_DIM:(g + 1) * HEAD_DIM],
                        preferred_element_type=F32) / l
            for r in range(rep):
                o_ref[rows, (g * rep + r) * HEAD_DIM:(g * rep + r + 1) * HEAD_DIM] = o[r * tq:(r + 1) * tq].astype(BF16)
        return carry

    lax.fori_loop(0, DSA_UNITS, attend_pass, 0)


def _dsa_prompt_call(p_p, kiwi_p, cfg, nb, t):
    m = p_p.shape[0]
    tb = DSA_UNITS * DSA_UNIT
    n_steps = t // tb
    k_top = min(TOPK_MAX, t // 4)
    qi_w = N_IDX_HEADS * D_IDX
    kvw = cfg.n_kv_dsa * HEAD_DIM
    qi_blk, qs_blk = cfg.off["qi"] // qi_w, cfg.off["q_s"] // cfg.d_dsa
    ks_blk, vs_blk = cfg.off["k_s"] // kvw, cfg.off["v_s"] // kvw
    n_widths = math.gcd(n_steps, 4)
    widths = tuple(t * (c + 1) // n_widths for c in range(n_widths))
    kern = functools.partial(_dsa_prompt_kernel, widths=widths, steps_per_width=n_steps // n_widths, k_top=k_top,
                             n_kv=cfg.n_kv_dsa, rep=cfg.rep_dsa)
    return pl.pallas_call(
        kern,
        grid=(nb, n_steps),
        in_specs=[pl.BlockSpec((tb, qi_w), lambda b, i: (b * n_steps + i, qi_blk)),
                  pl.BlockSpec((tb, LANES), lambda b, i: (b * n_steps + i, 0)),
                  pl.BlockSpec((tb, cfg.d_dsa), lambda b, i: (b * n_steps + i, qs_blk)),
                  pl.BlockSpec((t, LANES), lambda b, i: (b, 0)),
                  pl.BlockSpec((t, kvw), lambda b, i: (b, ks_blk)),
                  pl.BlockSpec((t, kvw), lambda b, i: (b, vs_blk))],
        out_specs=pl.BlockSpec((tb, cfg.d_dsa), lambda b, i: (b * n_steps + i, 0)),
        out_shape=jax.ShapeDtypeStruct((m, cfg.d_dsa), BF16),
        scratch_shapes=[pltpu.VMEM((t, LANES), BF16), pltpu.VMEM((t, kvw), BF16), pltpu.VMEM((t, kvw), BF16),
                        pltpu.VMEM((tb, t), F32), pltpu.VMEM((tb, t), F32)],
        compiler_params=_params(("arbitrary", "arbitrary"), 56),
        name="dsa_prompt",
    )(p_p, kiwi_p, p_p, kiwi_p, p_p, p_p)


def _mix_out_kernel(x_ref, hx_ref, gb_ref, gc_ref, ha_ref, hb_ref, cw_ref, yd_ref, ys_ref, z_ref,
                    mod_ref, wo_ref, o_ref, ut_ref, mix_sc, *, tm, seq_tiles, d_conv, d_diff, cols_outer):
    i = pl.program_id(1 if cols_outer else 0)
    j = pl.program_id(0 if cols_outer else 1)
    slot = i if cols_outer else 0
    if cols_outer:
        ut_ref[...] = (gc_ref[...] * hx_ref[...])[tm - SUBLANES:tm]

    @pl.when(j == 0)
    def _():
        u = gc_ref[...] * hx_ref[...]
        uh = ha_ref[...] * hb_ref[...]
        if seq_tiles:
            uh = jnp.where(i % seq_tiles == 0, 0.0, uh)
        row = lax.broadcasted_iota(jnp.int32, (tm, d_conv), 0)
        u1 = jnp.where(row == 0, uh[7:8], pltpu.roll(u, 1, 0))
        u2 = jnp.where(row == 0, uh[6:7], jnp.where(row == 1, uh[7:8], pltpu.roll(u, 2, 0)))
        cw = cw_ref[...]
        yc = gb_ref[...] * (cw[0:1] * u2 + cw[1:2] * u1 + cw[2:3] * u)
        if not cols_outer:
            ut_ref[...] = u[tm - SUBLANES:tm]

        def gated(y, c0, c1):
            zf = z_ref[:, c0:c1].astype(F32)
            mix_sc[slot, :, c0:c1] = (y * (zf * jax.nn.sigmoid(zf))).astype(mix_sc.dtype)

        gated(yc, 0, d_conv)
        gated(yd_ref[...].astype(F32), d_conv, d_conv + d_diff)
        gated(ys_ref[...].astype(F32), d_conv + d_diff, mix_sc.shape[2])

    mix = mix_sc[slot].astype(BF16)
    o_ref[...] = x_ref[...] + mod_ref[2:3, :] * jnp.dot(mix, wo_ref[...], preferred_element_type=F32)


def _mix_out_call(x2d, p_x, halo_a, halo_b, halo_map_a, halo_map_b, conv_w, y_diff, y_dsa, z_x, mod3,
                  wo16, layer, cfg, rows_per_batch, tm, seq_tiles, cols_outer, tn=512):
    m, d = x2d.shape
    dc = cfg.d_conv
    per = rows_per_batch // tm
    hx_blk, gb_blk, gc_blk = (cfg.off[n] // dc for n in ("hx", "gb", "gc"))
    kern = functools.partial(_mix_out_kernel, tm=tm, seq_tiles=seq_tiles, d_conv=dc, d_diff=cfg.d_diff,
                             cols_outer=cols_outer)

    def rc(index_map):
        return (lambda a, b: index_map(b, a)) if cols_outer else index_map

    mix_shape = (m // tm, tm, d) if cols_outer else (1, tm, d)
    return pl.pallas_call(
        kern,
        grid=(d // tn, m // tm) if cols_outer else (m // tm, d // tn),
        in_specs=[pl.BlockSpec((tm, tn), rc(lambda i, j: (i, j))),
                  pl.BlockSpec((tm, dc), rc(lambda i, j: (i, hx_blk))),
                  pl.BlockSpec((tm, dc), rc(lambda i, j: (i, gb_blk))),
                  pl.BlockSpec((tm, dc), rc(lambda i, j: (i, gc_blk))),
                  pl.BlockSpec((SUBLANES, dc), rc(halo_map_a)),
                  pl.BlockSpec((SUBLANES, dc), rc(halo_map_b)),
                  pl.BlockSpec(conv_w.shape, rc(lambda i, j: (0, 0))),
                  pl.BlockSpec((tm, cfg.d_diff), rc(lambda i, j: (i, 0))),
                  pl.BlockSpec((tm, cfg.d_dsa), rc(lambda i, j: (i, 0))),
                  pl.BlockSpec((tm, d), rc(lambda i, j: (i, 0))),
                  pl.BlockSpec((None, 3, tn), rc(lambda i, j: (i // per, 0, j))),
                  pl.BlockSpec((None, d, tn), rc(lambda i, j: (layer, 0, j)))],
        out_specs=[pl.BlockSpec((tm, tn), rc(lambda i, j: (i, j))),
                   pl.BlockSpec((None, None, SUBLANES, dc), rc(lambda i, j: (j if cols_outer else 0, i, 0, 0)))],
        out_shape=[jax.ShapeDtypeStruct((m, d), F32),
                   jax.ShapeDtypeStruct((d // tn if cols_outer else 1, m // tm, SUBLANES, dc), F32)],
        scratch_shapes=[pltpu.VMEM(mix_shape, F32 if cols_outer else BF16)],
        compiler_params=_params(("arbitrary", "arbitrary"), 56),
        name="mix_out",
    )(x2d, p_x, p_x, p_x, halo_a, halo_b, conv_w, y_diff, y_dsa, z_x, mod3, wo16)


ATTN_PAGES = 16
INDEX_PAGES = 32


def _page_specs(layer, pps, rows, width):
    def spec(p):
        return pl.BlockSpec((None, None, rows, width),
                            lambda b, s, pt: (layer, pt[b, s * pps + p], 0, 0))
    return [spec(p) for p in range(pps)]


def _cat_pages(refs, g, n_kv):
    return jnp.concatenate([r[pl.ds(g, PAGE_SIZE, stride=n_kv), :] for r in refs], axis=0).astype(BF16)


def _dec_diff_kernel(pt_ref, q_ref, *refs, pps, n_kv, rep, t_new, lam_init):
    kp, vp = refs[:pps], refs[pps:2 * pps]
    knew_ref, vnew_ref, lam_ref, sg_ref, o_ref, qs_sc, m_sc, l_sc, acc_sc = refs[2 * pps:]
    step = pl.program_id(1)
    rg = 2 * rep * SUBLANES
    groups = [slice(g * rg, (g + 1) * rg) for g in range(n_kv)]

    @pl.when(step == 0)
    def _():
        for h in range(n_kv * rep):
            qh = q_ref[:, h * HEAD_DIM:(h + 1) * HEAD_DIM] * ((HEAD_DIM // 2) ** -0.5)
            lo, hi = _half_masked(qh, SUBLANES)
            qs_sc[2 * h * SUBLANES:(2 * h + 1) * SUBLANES] = lo
            qs_sc[(2 * h + 1) * SUBLANES:(2 * h + 2) * SUBLANES] = hi
        _init_flash(m_sc, l_sc, acc_sc)

    scores = [_dot_nt(qs_sc[rows].astype(BF16), _cat_pages(kp, g, n_kv)) for g, rows in enumerate(groups)]
    for g, rows in enumerate(groups):
        _online_update(m_sc, l_sc, acc_sc, rows, scores[g], _cat_pages(vp, g, n_kv), False)

    @pl.when(step == pl.num_programs(1) - 1)
    def _():
        tok = lax.broadcasted_iota(jnp.int32, (rg, LANES), 0) % SUBLANES
        col = lax.broadcasted_iota(jnp.int32, (rg, LANES), 1)
        valid = (col <= tok) & (col < t_new)
        for g, rows in enumerate(groups):
            c0, c1 = g * HEAD_DIM, (g + 1) * HEAD_DIM
            s = jnp.where(valid, _dot_nt(qs_sc[rows].astype(BF16), knew_ref[:, c0:c1]), NEG_INF)
            _online_update(m_sc, l_sc, acc_sc, rows, s, vnew_ref[:, c0:c1], False)
        lmb = _lambda_full(lam_ref, lam_init)
        o_all = acc_sc[...] / l_sc[...]
        for h in range(n_kv * rep):
            o0 = o_all[2 * h * SUBLANES:(2 * h + 1) * SUBLANES]
            o1 = o_all[(2 * h + 1) * SUBLANES:(2 * h + 2) * SUBLANES]
            o_ref[:, h * HEAD_DIM:(h + 1) * HEAD_DIM] = _diff_head_out(o0, o1, lmb, sg_ref[...], lam_init)


def _flash_scratch(rows):
    return [pltpu.VMEM((rows, HEAD_DIM), F32), pltpu.VMEM((rows, LANES), F32),
            pltpu.VMEM((rows, LANES), F32), pltpu.VMEM((rows, HEAD_DIM), F32)]


def _dec_diff_call(page_table, p_s, cache_k, cache_v, knew16, vnew16, lam, sg, cfg, layer, t_new, lam_init):
    nb, n_pages = page_table.shape
    pps = math.gcd(n_pages, ATTN_PAGES)
    kvw = cfg.n_kv_diff * HEAD_DIM
    rows = cfg.n_kv_diff * 2 * cfg.rep_diff * SUBLANES
    q_blk = cfg.off["q_d"] // cfg.d_diff
    kern = functools.partial(_dec_diff_kernel, pps=pps, n_kv=cfg.n_kv_diff, rep=cfg.rep_diff,
                             t_new=t_new, lam_init=lam_init)
    grid_spec = pltpu.PrefetchScalarGridSpec(
        num_scalar_prefetch=1,
        grid=(nb, n_pages // pps),
        in_specs=[pl.BlockSpec((SUBLANES, cfg.d_diff), lambda b, s, pt: (b, q_blk))]
        + 2 * _page_specs(layer, pps, PAGE_SIZE * cfg.n_kv_diff, HEAD_DIM)
        + [pl.BlockSpec((None, PAGE_SIZE, kvw), lambda b, s, pt: (b, 0, 0)),
           pl.BlockSpec((None, PAGE_SIZE, kvw), lambda b, s, pt: (b, 0, 0)),
           pl.BlockSpec(lam.shape, lambda b, s, pt: (0, 0)),
           pl.BlockSpec((1, HEAD_DIM), lambda b, s, pt: (0, 0))],
        out_specs=pl.BlockSpec((SUBLANES, cfg.d_diff), lambda b, s, pt: (b, 0)),
        scratch_shapes=_flash_scratch(rows))
    return pl.pallas_call(
        kern,
        grid_spec=grid_spec,
        out_shape=jax.ShapeDtypeStruct((nb * SUBLANES, cfg.d_diff), F32),
        compiler_params=_params(("arbitrary", "arbitrary"), 48),
        name="diff_attn_sample",
    )(page_table, p_s, *([cache_k] * pps), *([cache_v] * pps), knew16, vnew16, lam, sg.reshape(1, HEAD_DIM))


def _dec_index_kernel(pt_ref, q_ref, w_ref, *refs, pps, t_new, past_len):
    kp = refs[:pps]
    knew_ref, sc_ref = refs[pps:]
    step = pl.program_id(1)
    span = pps * PAGE_SIZE

    def idx_score(s):
        s = jnp.maximum(s, 0.0) * (w_ref[...] * (N_IDX_HEADS ** -0.5))
        return jnp.sum(s.reshape(N_IDX_HEADS, SUBLANES, s.shape[-1]), axis=0)

    k_t = jnp.concatenate([r[...] for r in kp], axis=1).astype(BF16)
    off = pl.multiple_of(step * span, span)
    sc_ref[:, pl.ds(off, span)] = idx_score(jnp.dot(q_ref[...], k_t, preferred_element_type=F32))

    @pl.when(step == pl.num_programs(1) - 1)
    def _():
        tok = lax.broadcasted_iota(jnp.int32, (SUBLANES, LANES), 0)
        col = lax.broadcasted_iota(jnp.int32, (SUBLANES, LANES), 1)
        valid = (col <= tok) & (col < t_new)
        sc_ref[:, past_len:past_len + LANES] = jnp.where(valid, idx_score(_dot_nt(q_ref[...], knew_ref[...])), NEG_INF)


def _select_kernel(s_ref, o_ref, *, k_top):
    o_ref[...] = _selection_bias(s_ref[...], k_top)


def _select_call(scores, k_top):
    r, w = scores.shape
    return pl.pallas_call(
        functools.partial(_select_kernel, k_top=k_top),
        grid=(1,),
        in_specs=[pl.BlockSpec((r, w), lambda i: (0, 0))],
        out_specs=pl.BlockSpec((r, w), lambda i: (0, 0)),
        out_shape=jax.ShapeDtypeStruct((r, w), F32),
        compiler_params=_params(("arbitrary",), 56),
        name="dsa_select_sample",
    )(scores)


def _dec_index_call(page_table, qi_rows, wi_col, cache_kidx_t, kinew16, layer, t_new):
    nb, n_pages = page_table.shape
    pps = math.gcd(n_pages, INDEX_PAGES)
    past_len = n_pages * PAGE_SIZE
    k_top = min(TOPK_MAX, (past_len + t_new) // 4)
    w = past_len + LANES
    rows = N_IDX_HEADS * SUBLANES
    kern = functools.partial(_dec_index_kernel, pps=pps, t_new=t_new, past_len=past_len)
    grid_spec = pltpu.PrefetchScalarGridSpec(
        num_scalar_prefetch=1,
        grid=(nb, n_pages // pps),
        in_specs=[pl.BlockSpec((None, rows, D_IDX), lambda b, s, pt: (b, 0, 0)),
                  pl.BlockSpec((None, rows, 1), lambda b, s, pt: (b, 0, 0))]
        + _page_specs(layer, pps, D_IDX, PAGE_SIZE)
        + [pl.BlockSpec((None, PAGE_SIZE, D_IDX), lambda b, s, pt: (b, 0, 0))],
        out_specs=pl.BlockSpec((None, SUBLANES, w), lambda b, s, pt: (b, 0, 0)))
    scores = pl.pallas_call(
        kern,
        grid_spec=grid_spec,
        out_shape=jax.ShapeDtypeStruct((nb, SUBLANES, w), F32),
        compiler_params=_params(("arbitrary", "arbitrary"), 40),
        name="dsa_index_sample",
    )(page_table, qi_rows, wi_col, *([cache_kidx_t] * pps), kinew16)
    return _select_call(scores.reshape(nb * SUBLANES, w), k_top).reshape(nb, SUBLANES, w)


def _dec_dsa_kernel(pt_ref, q_ref, bias_ref, bnew_ref, *refs, pps, n_kv, rep):
    kp, vp = refs[:pps], refs[pps:2 * pps]
    knew_ref, vnew_ref, o_ref, qs_sc, m_sc, l_sc, acc_sc = refs[2 * pps:]
    step = pl.program_id(1)
    rg = rep * SUBLANES
    groups = [slice(g * rg, (g + 1) * rg) for g in range(n_kv)]
    scale = HEAD_DIM ** -0.5

    @pl.when(step == 0)
    def _():
        for h in range(n_kv * rep):
            qs_sc[h * SUBLANES:(h + 1) * SUBLANES] = q_ref[:, h * HEAD_DIM:(h + 1) * HEAD_DIM]
        _init_flash(m_sc, l_sc, acc_sc)

    def scores(g, k16, bias):
        s = _dot_nt(qs_sc[groups[g]].astype(BF16), k16) * scale
        n = s.shape[-1]
        return (s.reshape(rep, SUBLANES, n) + bias[None]).reshape(rg, n)

    s_all = [scores(g, _cat_pages(kp, g, n_kv), bias_ref[...]) for g in range(n_kv)]
    for g in range(n_kv):
        _online_update(m_sc, l_sc, acc_sc, groups[g], s_all[g], _cat_pages(vp, g, n_kv), True)

    @pl.when(step == pl.num_programs(1) - 1)
    def _():
        for g in range(n_kv):
            c0, c1 = g * HEAD_DIM, (g + 1) * HEAD_DIM
            _online_update(m_sc, l_sc, acc_sc, groups[g], scores(g, knew_ref[:, c0:c1], bnew_ref[...]),
                           vnew_ref[:, c0:c1], True)
        o_all = acc_sc[...] / l_sc[...]
        for h in range(n_kv * rep):
            o_ref[:, h * HEAD_DIM:(h + 1) * HEAD_DIM] = o_all[h * SUBLANES:(h + 1) * SUBLANES]


def _dec_dsa_call(page_table, p_s, bias, cache_k, cache_v, knew16, vnew16, cfg, layer):
    nb, n_pages = page_table.shape
    pps = math.gcd(n_pages, ATTN_PAGES)
    kvw = cfg.n_kv_dsa * HEAD_DIM
    rows = cfg.n_kv_dsa * cfg.rep_dsa * SUBLANES
    span = pps * PAGE_SIZE
    new_blk = n_pages * PAGE_SIZE // LANES
    q_blk = cfg.off["q_s"] // cfg.d_dsa
    kern = functools.partial(_dec_dsa_kernel, pps=pps, n_kv=cfg.n_kv_dsa, rep=cfg.rep_dsa)
    grid_spec = pltpu.PrefetchScalarGridSpec(
        num_scalar_prefetch=1,
        grid=(nb, n_pages // pps),
        in_specs=[pl.BlockSpec((SUBLANES, cfg.d_dsa), lambda b, s, pt: (b, q_blk)),
                  pl.BlockSpec((None, SUBLANES, span), lambda b, s, pt: (b, 0, s)),
                  pl.BlockSpec((None, SUBLANES, LANES), lambda b, s, pt: (b, 0, new_blk))]
        + 2 * _page_specs(layer, pps, PAGE_SIZE * cfg.n_kv_dsa, HEAD_DIM)
        + [pl.BlockSpec((None, PAGE_SIZE, kvw), lambda b, s, pt: (b, 0, 0)),
           pl.BlockSpec((None, PAGE_SIZE, kvw), lambda b, s, pt: (b, 0, 0))],
        out_specs=pl.BlockSpec((SUBLANES, cfg.d_dsa), lambda b, s, pt: (b, 0)),
        scratch_shapes=_flash_scratch(rows))
    return pl.pallas_call(
        kern,
        grid_spec=grid_spec,
        out_shape=jax.ShapeDtypeStruct((nb * SUBLANES, cfg.d_dsa), F32),
        compiler_params=_params(("arbitrary", "arbitrary"), 40),
        name="dsa_attn_sample",
    )(page_table, p_s, bias, bias, *([cache_k] * pps), *([cache_v] * pps), knew16, vnew16)


def _pad_rows(a, rows):
    return jnp.pad(a, ((0, 0), (0, rows - a.shape[1]), (0, 0)))


def kernel(x_prompt, x_sample, cache_diff_k, cache_diff_v, cache_dsa_k, cache_dsa_v, cache_dsa_kidx,
           state_conv, page_table, c_prompt, c_sample, norm_g, w_ada, b_ada, w_in, conv_w,
           diff_lam, diff_subln_g, w_out, final_g):
    nbp, t, d = x_prompt.shape
    nbs, t_new, _ = x_sample.shape
    depth, n_pool = cache_diff_k.shape[:2]
    cfg = _make_cfg(d, cache_diff_k.shape[3], cache_dsa_k.shape[3])
    dc = cfg.d_conv
    off = cfg.off
    kdw, ksw = cfg.n_kv_diff * HEAD_DIM, cfg.n_kv_dsa * HEAD_DIM
    kiwi_w = D_IDX + N_IDX_HEADS
    assert cache_diff_k.shape[2] == PAGE_SIZE and cache_diff_k.shape[4] == HEAD_DIM
    assert state_conv.shape[2] == CONV_WIDTH - 1 and CONV_WIDTH - 1 <= t_new <= SUBLANES
    assert w_in.shape[2] == cfg.n_main + kiwi_w + d and cfg.d_diff == cfg.d_dsa

    w_main16 = w_in.astype(BF16)
    w_kiwi16 = jnp.pad(w_in[:, :, cfg.n_main:cfg.n_main + kiwi_w], ((0, 0), (0, 0), (0, LANES - kiwi_w))).astype(BF16)
    w_z16 = w_in[:, :, cfg.n_main + kiwi_w:].astype(BF16)
    wo16 = w_out.astype(BF16)
    z_perm = tuple(range(d // PROJ_TN))

    n_c = nbp + nbs
    c_all = _pad_rows(jnp.concatenate([c_prompt, c_sample], axis=0)[None], -(-n_c // SUBLANES) * SUBLANES)[0]
    mod = _mod_call(c_all, w_ada, b_ada).reshape(depth, c_all.shape[0], 3, d)

    ck_d = cache_diff_k.reshape(depth, n_pool, PAGE_SIZE * cfg.n_kv_diff, HEAD_DIM)
    cv_d = cache_diff_v.reshape(depth, n_pool, PAGE_SIZE * cfg.n_kv_diff, HEAD_DIM)
    ck_s = cache_dsa_k.reshape(depth, n_pool, PAGE_SIZE * cfg.n_kv_dsa, HEAD_DIM)
    cv_s = cache_dsa_v.reshape(depth, n_pool, PAGE_SIZE * cfg.n_kv_dsa, HEAD_DIM)
    ck_i = jnp.swapaxes(cache_dsa_kidx, 2, 3)

    xp = x_prompt.reshape(nbp * t, d)
    xs = _pad_rows(x_sample, SUBLANES).reshape(nbs * SUBLANES, d)
    ones_halo = jnp.ones((SUBLANES, dc), F32)
    tm_p = 512
    seq_tiles = t // tm_p
    halo_blocks = tm_p // SUBLANES
    ms = nbs * SUBLANES

    def prompt_halo(name):
        col_blk = off[name] // dc
        return lambda i, j: (jnp.maximum(i * halo_blocks - 1, 0), col_blk)

    def project(h16, l, tm, z_dtype):
        p = _matmul_call(h16, w_main16, l, cfg.perm, F32, tm, PROJ_TN)
        z = _matmul_call(h16, w_z16, l, z_perm, z_dtype, tm, PROJ_TN)
        kiwi = _matmul_call(h16, w_kiwi16, l, (0,), F32, tm, LANES)
        return p, z, kiwi

    p_rows, s_rows = [], []
    for l in range(depth):
        lam_init = LAMBDA_INIT_BASE - LAMBDA_INIT_AMP * math.exp(-LAMBDA_INIT_RATE * l)
        mod_p, mod_s = mod[l, :nbp], mod[l, nbp:n_c]

        h_p = _norm_mod_call(xp, norm_g[l], mod_p, t, 256, BF16)
        p_p, z_p, kiwi_p = project(h_p, l, 1024, BF16)
        yd_p = _diff_prompt_call(p_p, diff_lam[l], diff_subln_g[l], cfg, nbp, t, lam_init)
        ys_p = _dsa_prompt_call(p_p, kiwi_p, cfg, nbp, t)
        xp, ut_p = _mix_out_call(xp, p_p, p_p, p_p, prompt_halo("hx"), prompt_halo("gc"), conv_w[l], yd_p, ys_p,
                                 z_p, mod_p, wo16, l, cfg, t, tm_p, seq_tiles, False)
        p_rows.append((p_p, kiwi_p,
                       ut_p.reshape(nbp, seq_tiles, SUBLANES, dc)[:, -1, SUBLANES - (CONV_WIDTH - 1):]))

        h_s = _norm_mod_call(xs, norm_g[l], mod_s, SUBLANES, SUBLANES, F32).astype(BF16)
        p_s, z_s, kiwi_s = project(h_s, l, ms, F32)
        p_s3 = p_s.reshape(nbs, SUBLANES, cfg.n_main)
        kiwi_s3 = kiwi_s.reshape(nbs, SUBLANES, LANES)

        def new_rows(name, width):
            return _pad_rows(p_s3[:, :, off[name]:off[name] + width], PAGE_SIZE).astype(BF16)

        yd_s = _dec_diff_call(page_table, p_s, ck_d, cv_d, new_rows("k_d", kdw), new_rows("v_d", kdw),
                              diff_lam[l], diff_subln_g[l], cfg, l, t_new, lam_init)
        qi_rows = p_s3[:, :, off["qi"]:off["qi"] + N_IDX_HEADS * D_IDX].reshape(nbs, SUBLANES, N_IDX_HEADS, D_IDX)
        qi_rows = qi_rows.transpose(0, 2, 1, 3).reshape(nbs, N_IDX_HEADS * SUBLANES, D_IDX).astype(BF16)
        wi_col = kiwi_s3[:, :, D_IDX:kiwi_w].transpose(0, 2, 1).reshape(nbs, N_IDX_HEADS * SUBLANES, 1)
        ki_new = _pad_rows(kiwi_s3[:, :, :D_IDX], PAGE_SIZE).astype(BF16)
        bias = _dec_index_call(page_table, qi_rows, wi_col, ck_i, ki_new, l, t_new)
        ys_s = _dec_dsa_call(page_table, p_s, bias, ck_s, cv_s, new_rows("k_s", ksw), new_rows("v_s", ksw),
                             cfg, l)
        state_pad = jnp.pad(state_conv[l], ((0, 0), (SUBLANES - (CONV_WIDTH - 1), 0), (0, 0)))
        xs, ut_s = _mix_out_call(xs, p_s, state_pad.reshape(ms, dc), ones_halo,
                                 lambda i, j: (i, 0), lambda i, j: (0, 0), conv_w[l], yd_s, ys_s,
                                 z_s, mod_s, wo16, l, cfg, SUBLANES, SUBLANES, 0, True)
        s_rows.append((p_s3[:, :t_new], kiwi_s3[:, :t_new], ut_s[0, :, t_new - (CONV_WIDTH - 1):t_new]))

    y_prompt = _rmsnorm_call(xp, final_g, 256).reshape(nbp, t, d)
    y_sample = _rmsnorm_call(xs, final_g, SUBLANES).reshape(nbs, SUBLANES, d)[:, :t_new]

    def stack_p(name, n_kv):
        return jnp.stack([r[0][:, off[name]:off[name] + n_kv * HEAD_DIM].reshape(nbp, t, n_kv, HEAD_DIM)
                          for r in p_rows])

    def stack_s(name, n_kv):
        return jnp.stack([r[0][:, :, off[name]:off[name] + n_kv * HEAD_DIM].reshape(nbs, t_new, n_kv, HEAD_DIM)
                          for r in s_rows])

    return (y_prompt, y_sample,
            stack_p("k_d", cfg.n_kv_diff), stack_p("v_d", cfg.n_kv_diff),
            stack_p("k_s", cfg.n_kv_dsa), stack_p("v_s", cfg.n_kv_dsa),
            jnp.stack([r[1][:, :D_IDX].reshape(nbp, t, D_IDX) for r in p_rows]),
            jnp.stack([r[2] for r in p_rows]),
            stack_s("k_d", cfg.n_kv_diff), stack_s("v_d", cfg.n_kv_diff),
            stack_s("k_s", cfg.n_kv_dsa), stack_s("v_s", cfg.n_kv_dsa),
            jnp.stack([r[1][:, :, :D_IDX] for r in s_rows]),
            jnp.stack([r[2] for r in s_rows]))
```

```python
import functools
import math
from typing import NamedTuple

import jax
import jax.numpy as jnp
from jax import lax
from jax.experimental import pallas as pl
from jax.experimental.pallas import tpu as pltpu

F32 = jnp.float32
BF16 = jnp.bfloat16

HEAD_DIM = 128
D_IDX = 64
N_IDX_HEADS = 16
CONV_WIDTH = 3
TOPK_MAX = 256
PAGE_SIZE = 128
RMS_EPS = 1e-6
LAMBDA_INIT_BASE = 0.8
LAMBDA_INIT_AMP = 0.6
LAMBDA_INIT_RATE = 0.3

LANES = 128
SUBLANES = 8
MIB = 2 ** 20
PROJ_TN = 512

NEG_INF = float("-inf")
INT_MIN = -(2 ** 31)
KEY_NEG_INF = INT_MIN + 0x7FFFFF


class Cfg(NamedTuple):
    d_model: int
    d_conv: int
    n_kv_diff: int
    rep_diff: int
    d_diff: int
    n_kv_dsa: int
    rep_dsa: int
    d_dsa: int
    n_main: int
    off: dict
    perm: tuple


def _make_cfg(d_model, n_kv_diff, n_kv_dsa):
    d_conv = d_model // 4
    d_diff = 3 * d_model // 8
    d_dsa = 3 * d_model // 8
    kd, ks = n_kv_diff * HEAD_DIM, n_kv_dsa * HEAD_DIM
    qi = N_IDX_HEADS * D_IDX
    w_order = (("hx", d_conv), ("gb", d_conv), ("gc", d_conv), ("q_d", d_diff), ("k_d", kd), ("v_d", kd),
               ("q_s", d_dsa), ("kv_s", 2 * ks), ("qi", qi))
    out_order = ("q_d", "q_s", "qi", "hx", "gb", "gc", "k_d", "v_d", "kv_s")
    width = dict(w_order)
    off, pos = {}, 0
    for name in out_order:
        assert width[name] % PROJ_TN == 0
        off[name] = pos
        pos += width[name]
    off["k_s"], off["v_s"] = off["kv_s"], off["kv_s"] + ks
    perm = []
    for name, wd in w_order:
        perm.extend(off[name] // PROJ_TN + b for b in range(wd // PROJ_TN))
    return Cfg(d_model, d_conv, n_kv_diff, d_diff // HEAD_DIM // n_kv_diff, d_diff,
               n_kv_dsa, d_dsa // HEAD_DIM // n_kv_dsa, d_dsa, pos, off, tuple(perm))


def _params(semantics, vmem_mib):
    return pltpu.CompilerParams(dimension_semantics=semantics, vmem_limit_bytes=vmem_mib * MIB)


def _dot_nt(a, b):
    return lax.dot_general(a, b, (((1,), (1,)), ((), ())), preferred_element_type=F32)


MOD_STREAMS = 8
MOD_ROWS = 32


def _mod_kernel(c_ref, *refs):
    w_refs, (b_ref, o_ref) = refs[:MOD_STREAMS], refs[MOD_STREAMS:]

    @pl.when(pl.program_id(1) == 0)
    def _():
        o_ref[...] = jnp.broadcast_to(b_ref[...], o_ref.shape)

    c = c_ref[...]
    a = (c * jax.nn.sigmoid(c)).astype(BF16)
    w = jnp.concatenate([r[...].astype(BF16) for r in w_refs], axis=0)
    o_ref[...] += jnp.dot(a, w, preferred_element_type=F32)


def _mod_call(c_all, w_ada, b_ada):
    depth, d, n3 = w_ada.shape
    rows = c_all.shape[0]
    tk = MOD_STREAMS * MOD_ROWS

    def w_spec(p):
        return pl.BlockSpec((None, MOD_ROWS, n3), lambda l, k: (l, k * MOD_STREAMS + p, 0))

    return pl.pallas_call(
        _mod_kernel,
        grid=(depth, d // tk),
        in_specs=[pl.BlockSpec((rows, tk), lambda l, k: (0, k))]
        + [w_spec(p) for p in range(MOD_STREAMS)]
        + [pl.BlockSpec((None, 1, n3), lambda l, k: (l, 0, 0))],
        out_specs=pl.BlockSpec((None, rows, n3), lambda l, k: (l, 0, 0)),
        out_shape=jax.ShapeDtypeStruct((depth, rows, n3), F32),
        compiler_params=_params(("arbitrary", "arbitrary"), 48),
        name="adaln_mod",
    )(c_all, *([w_ada] * MOD_STREAMS), b_ada.reshape(depth, 1, n3))


def _rms(x):
    return x * lax.rsqrt(jnp.mean(x * x, axis=-1, keepdims=True) + RMS_EPS)


NORM_STREAMS = 1


def _row_streams(tm):
    return NORM_STREAMS if tm % (NORM_STREAMS * SUBLANES) == 0 else 1


def _row_stream_specs(tm, d, streams):
    rows = tm // streams

    def spec(p):
        return pl.BlockSpec((rows, d), lambda i: (i * streams + p, 0))
    return [spec(p) for p in range(streams)]


def _norm_mod_kernel(*refs):
    x_refs, (g_ref, mod_ref, o_ref) = refs[:-3], refs[-3:]
    x = jnp.concatenate([r[...] for r in x_refs], axis=0)
    y = _rms(x) * g_ref[...]
    o_ref[...] = (y * (1.0 + mod_ref[1:2, :]) + mod_ref[0:1, :]).astype(o_ref.dtype)


def _norm_mod_call(x2d, g, mod3, rows_per_batch, tm, out_dtype):
    m, d = x2d.shape
    per = rows_per_batch // tm
    streams = _row_streams(tm)
    return pl.pallas_call(
        _norm_mod_kernel,
        grid=(m // tm,),
        in_specs=_row_stream_specs(tm, d, streams)
        + [pl.BlockSpec((1, d), lambda i: (0, 0)),
           pl.BlockSpec((None, 3, d), lambda i: (i // per, 0, 0))],
        out_specs=pl.BlockSpec((tm, d), lambda i: (i, 0)),
        out_shape=jax.ShapeDtypeStruct((m, d), out_dtype),
        compiler_params=_params(("arbitrary",), 40),
        name="norm_mod",
    )(*([x2d] * streams), g.reshape(1, d), mod3)


def _rmsnorm_kernel(*refs):
    x_refs, (g_ref, o_ref) = refs[:-2], refs[-2:]
    o_ref[...] = _rms(jnp.concatenate([r[...] for r in x_refs], axis=0)) * g_ref[...]


def _rmsnorm_call(x2d, g, tm):
    m, d = x2d.shape
    streams = _row_streams(tm)
    return pl.pallas_call(
        _rmsnorm_kernel,
        grid=(m // tm,),
        in_specs=_row_stream_specs(tm, d, streams) + [pl.BlockSpec((1, d), lambda i: (0, 0))],
        out_specs=pl.BlockSpec((tm, d), lambda i: (i, 0)),
        out_shape=jax.ShapeDtypeStruct((m, d), F32),
        compiler_params=_params(("arbitrary",), 40),
        name="final_norm",
    )(*([x2d] * streams), g.reshape(1, d))


def _matmul_kernel(perm_ref, x_ref, w_ref, o_ref):
    o_ref[...] = jnp.dot(x_ref[...], w_ref[...], preferred_element_type=F32).astype(o_ref.dtype)


def _matmul_call(x16, w16, layer, perm, out_dtype, tm, tn):
    m, k = x16.shape
    n_blocks = len(perm)
    grid_spec = pltpu.PrefetchScalarGridSpec(
        num_scalar_prefetch=1,
        grid=(m // tm, n_blocks),
        in_specs=[pl.BlockSpec((tm, k), lambda i, j, pm: (i, 0)),
                  pl.BlockSpec((None, k, tn), lambda i, j, pm: (layer, 0, j))],
        out_specs=pl.BlockSpec((tm, tn), lambda i, j, pm: (i, pm[j])))
    return pl.pallas_call(
        _matmul_kernel,
        grid_spec=grid_spec,
        out_shape=jax.ShapeDtypeStruct((m, n_blocks * tn), out_dtype),
        compiler_params=_params(("arbitrary", "arbitrary"), 48),
        name="in_proj",
    )(jnp.asarray(perm, jnp.int32), x16, w16)


def _half_masked(q, rows):
    lane = lax.broadcasted_iota(jnp.int32, (rows, HEAD_DIM), 1)
    zero = jnp.zeros_like(q)
    return jnp.where(lane < HEAD_DIM // 2, q, zero), jnp.where(lane >= HEAD_DIM // 2, q, zero)


def _lambda_full(lam_ref, lam_init):
    lam = lam_ref[...]
    a = jnp.sum(lam[0:1, :] * lam[1:2, :], axis=-1, keepdims=True)
    b = jnp.sum(lam[2:3, :] * lam[3:4, :], axis=-1, keepdims=True)
    return jnp.exp(a) - jnp.exp(b) + lam_init


def _diff_head_out(o0, o1, lmb, sg, lam_init):
    o = o0 - lmb * o1
    return _rms(o) * sg * (1.0 - lam_init)


def _online_update(m_sc, l_sc, acc_sc, rows, s, v16, guard_empty):
    m_prev = m_sc[rows]
    m_new = jnp.maximum(m_prev, jnp.max(s, axis=-1, keepdims=True))
    m_use = jnp.where(m_new == NEG_INF, 0.0, m_new) if guard_empty else m_new
    alpha = jnp.exp(m_prev - m_use)
    p = jnp.exp(s - jnp.concatenate([m_use] * (s.shape[-1] // LANES), axis=1))
    l_sc[rows] = alpha * l_sc[rows] + jnp.sum(p, axis=-1, keepdims=True)
    acc_sc[rows] = alpha * acc_sc[rows] + jnp.dot(p.astype(BF16), v16, preferred_element_type=F32)
    m_sc[rows] = m_new


def _init_flash(m_sc, l_sc, acc_sc):
    m_sc[...] = jnp.full_like(m_sc, NEG_INF)
    l_sc[...] = jnp.zeros_like(l_sc)
    acc_sc[...] = jnp.zeros_like(acc_sc)


def _diff_prompt_kernel(q_ref, k_ref, v_ref, lam_ref, sg_ref, o_ref, qs_sc, m_sc, l_sc, acc_sc,
                        *, rep, tq, lam_init):
    i = pl.program_id(2)
    chunk = LANES
    n_chunks = 2 * rep * tq // chunk
    for r in range(rep):
        qr = q_ref[:, r * HEAD_DIM:(r + 1) * HEAD_DIM] * ((HEAD_DIM // 2) ** -0.5)
        lo, hi = _half_masked(qr, tq)
        qs_sc[(2 * r) * tq:(2 * r + 1) * tq] = lo.astype(BF16)
        qs_sc[(2 * r + 1) * tq:(2 * r + 2) * tq] = hi.astype(BF16)
    _init_flash(m_sc, l_sc, acc_sc)

    def kv_block(j, diagonal):
        start = pl.multiple_of(j * tq, tq)
        k = k_ref[pl.ds(start, tq), :].astype(BF16)
        v = v_ref[pl.ds(start, tq), :].astype(BF16)
        for c in range(n_chunks):
            rows = slice(c * chunk, (c + 1) * chunk)
            s = _dot_nt(qs_sc[rows], k)
            if diagonal:
                q0 = (c * chunk) % tq
                row = lax.broadcasted_iota(jnp.int32, (chunk, tq), 0) + q0
                col = lax.broadcasted_iota(jnp.int32, (chunk, tq), 1)
                s = jnp.where(col <= row, s, NEG_INF)
            _online_update(m_sc, l_sc, acc_sc, rows, s, v, False)

    def full_block(j, carry):
        kv_block(j, False)
        return carry

    lax.fori_loop(0, i, full_block, 0)
    kv_block(i, True)

    lmb = _lambda_full(lam_ref, lam_init)
    for r in range(rep):
        lo = slice((2 * r) * tq, (2 * r + 1) * tq)
        hi = slice((2 * r + 1) * tq, (2 * r + 2) * tq)
        y = _diff_head_out(acc_sc[lo] / l_sc[lo], acc_sc[hi] / l_sc[hi], lmb, sg_ref[...], lam_init)
        o_ref[:, r * HEAD_DIM:(r + 1) * HEAD_DIM] = y.astype(BF16)


def _diff_prompt_call(p_p, lam, sg, cfg, nb, t, lam_init, tq=256):
    m = p_p.shape[0]
    nq = t // tq
    g_w = cfg.rep_diff * HEAD_DIM
    rows = 2 * cfg.rep_diff * tq
    q_blk, k_blk, v_blk = cfg.off["q_d"] // g_w, cfg.off["k_d"] // HEAD_DIM, cfg.off["v_d"] // HEAD_DIM
    kern = functools.partial(_diff_prompt_kernel, rep=cfg.rep_diff, tq=tq, lam_init=lam_init)
    return pl.pallas_call(
        kern,
        grid=(nb, cfg.n_kv_diff, nq),
        in_specs=[pl.BlockSpec((tq, g_w), lambda b, g, i: (b * nq + i, q_blk + g)),
                  pl.BlockSpec((t, HEAD_DIM), lambda b, g, i: (b, k_blk + g)),
                  pl.BlockSpec((t, HEAD_DIM), lambda b, g, i: (b, v_blk + g)),
                  pl.BlockSpec(lam.shape, lambda b, g, i: (0, 0)),
                  pl.BlockSpec((1, HEAD_DIM), lambda b, g, i: (0, 0))],
        out_specs=pl.BlockSpec((tq, g_w), lambda b, g, i: (b * nq + i, g)),
        out_shape=jax.ShapeDtypeStruct((m, cfg.d_diff), BF16),
        scratch_shapes=[pltpu.VMEM((rows, HEAD_DIM), BF16), pltpu.VMEM((rows, LANES), F32),
                        pltpu.VMEM((rows, LANES), F32), pltpu.VMEM((rows, HEAD_DIM), F32)],
        compiler_params=_params(("arbitrary", "arbitrary", "arbitrary"), 40),
        name="diff_attn_prompt",
    )(p_p, p_p, p_p, lam, sg.reshape(1, HEAD_DIM))


def _monotone_key(x):
    bits = lax.bitcast_convert_type(x, jnp.int32)
    key = bits ^ ((bits >> 31) & jnp.int32(0x7FFFFFFF))
    return jnp.where(bits == jnp.int32(INT_MIN), 0, key)


def _count(mask):
    return jnp.sum(jnp.where(mask, 1.0, 0.0), axis=-1, keepdims=True)


def _topk_mask(score, k_top):
    r, w = score.shape
    key = _monotone_key(score)
    kf = float(k_top)

    def value_bit(it, t):
        cand = t + (jnp.int32(1) << (31 - it))
        return jnp.where(_count(key >= cand) >= kf, cand, t)

    t = lax.fori_loop(0, 32, value_bit, jnp.full((r, 1), INT_MIN, jnp.int32))
    gt = key > t
    eq = key == t
    n_gt = _count(gt)
    need = kf - n_gt
    col = lax.broadcasted_iota(jnp.int32, (r, w), 1)
    ambiguous = (n_gt + _count(eq) > kf) & (t > KEY_NEG_INF)
    n_bits = w.bit_length()

    def tie_search(_):
        def index_bit(it, j):
            cand = j + (jnp.int32(1) << (n_bits - 1 - it))
            return jnp.where(_count(eq & (col < cand)) < need, cand, j)
        return lax.fori_loop(0, n_bits, index_bit, jnp.zeros((r, 1), jnp.int32))

    any_amb = jnp.max(jnp.where(ambiguous, 1.0, 0.0)) > 0.5
    j = lax.cond(any_amb, tie_search, lambda _: jnp.full((r, 1), w, jnp.int32), 0)
    return gt | (eq & (col <= j))


def _selection_bias(score, k_top):
    sel = _topk_mask(score, k_top) & (score > NEG_INF)
    return jnp.where(sel, 0.0, NEG_INF)


DSA_UNIT = 128
DSA_UNITS = 2


def _dsa_prompt_kernel(qi_ref, wi_ref, qs_ref, ki_ref, ks_ref, vs_ref, o_ref, kiki_sc, ks_sc, vs_sc,
                       score_sc, bias_sc, *, widths, steps_per_width, k_top, n_kv, rep):
    i = pl.program_id(1)

    @pl.when(i == 0)
    def _():
        ki = ki_ref[...]
        lane = lax.broadcasted_iota(jnp.int32, ki.shape, 1)
        kiki_sc[...] = jnp.where(lane < D_IDX, ki, pltpu.roll(ki, D_IDX, 1)).astype(BF16)
        ks_sc[...] = ks_ref[...].astype(BF16)
        vs_sc[...] = vs_ref[...].astype(BF16)

    for c, w in enumerate(widths):
        pl.when(i // steps_per_width == c)(
            functools.partial(_dsa_prompt_body, qi_ref, wi_ref, qs_ref, kiki_sc, ks_sc, vs_sc, o_ref, score_sc,
                              bias_sc, i=i, w=w, k_top=k_top, n_kv=n_kv, rep=rep))


def _dsa_prompt_body(qi_ref, wi_ref, qs_ref, kiki_sc, ks_sc, vs_sc, o_ref, score_sc, bias_sc,
                     *, i, w, k_top, n_kv, rep):
    tq = DSA_UNIT
    first_row = i * (DSA_UNITS * tq)
    kiki = kiki_sc[0:w, :]

    def index_pass(u, carry):
        r0 = pl.multiple_of(u * tq, tq)
        rows = pl.ds(r0, tq)
        wi = wi_ref[rows, :][:, D_IDX:D_IDX + N_IDX_HEADS] * (N_IDX_HEADS ** -0.5)
        idx = jnp.zeros((tq, w), F32)
        for j in range(N_IDX_HEADS // 2):
            lo, hi = _half_masked(qi_ref[rows, j * LANES:(j + 1) * LANES], tq)
            sc = jnp.maximum(_dot_nt(jnp.concatenate([lo, hi], axis=0).astype(BF16), kiki), 0.0)
            idx = idx + sc[:tq] * wi[:, 2 * j:2 * j + 1] + sc[tq:] * wi[:, 2 * j + 1:2 * j + 2]
        row = lax.broadcasted_iota(jnp.int32, (tq, w), 0) + (first_row + r0)
        col = lax.broadcasted_iota(jnp.int32, (tq, w), 1)
        score_sc[rows, 0:w] = jnp.where(col <= row, idx, NEG_INF)
        return carry

    lax.fori_loop(0, DSA_UNITS, index_pass, 0)
    bias_sc[:, 0:w] = _selection_bias(score_sc[:, 0:w], k_top)

    def attend_pass(u, carry):
        rows = pl.ds(pl.multiple_of(u * tq, tq), tq)
        bias = bias_sc[rows, 0:w]
        for g in range(n_kv):
            heads = [qs_ref[rows, (g * rep + r) * HEAD_DIM:(g * rep + r + 1) * HEAD_DIM] for r in range(rep)]
            q = jnp.concatenate(heads, axis=0).astype(BF16)
            s = _dot_nt(q, ks_sc[0:w, g * HEAD_DIM:(g + 1) * HEAD_DIM]) * (HEAD_DIM ** -0.5)
            s = s.reshape(rep, tq, w) + bias[None]
            p = jnp.exp(s - jnp.max(s, axis=-1, keepdims=True))
            l = jnp.sum(p, axis=-1, keepdims=True).reshape(rep * tq, 1)
            o = jnp.dot(p.reshape(rep * tq, w).astype(BF16), vs_sc[0:w, g * HEAD_DIM:(g + 1) * HEAD_DIM],
                        preferred_element_type=F32) / l
            for r in range(rep):
                o_ref[rows, (g * rep + r) * HEAD_DIM:(g * rep + r + 1) * HEAD_DIM] = o[r * tq:(r + 1) * tq].astype(BF16)
        return carry

    lax.fori_loop(0, DSA_UNITS, attend_pass, 0)


def _dsa_prompt_call(p_p, kiwi_p, cfg, nb, t):
    m = p_p.shape[0]
    tb = DSA_UNITS * DSA_UNIT
    n_steps = t // tb
    k_top = min(TOPK_MAX, t // 4)
    qi_w = N_IDX_HEADS * D_IDX
    kvw = cfg.n_kv_dsa * HEAD_DIM
    qi_blk, qs_blk = cfg.off["qi"] // qi_w, cfg.off["q_s"] // cfg.d_dsa
    ks_blk, vs_blk = cfg.off["k_s"] // kvw, cfg.off["v_s"] // kvw
    n_widths = math.gcd(n_steps, 8)
    widths = tuple(t * (c + 1) // n_widths for c in range(n_widths))
    kern = functools.partial(_dsa_prompt_kernel, widths=widths, steps_per_width=n_steps // n_widths, k_top=k_top,
                             n_kv=cfg.n_kv_dsa, rep=cfg.rep_dsa)
    return pl.pallas_call(
        kern,
        grid=(nb, n_steps),
        in_specs=[pl.BlockSpec((tb, qi_w), lambda b, i: (b * n_steps + i, qi_blk)),
                  pl.BlockSpec((tb, LANES), lambda b, i: (b * n_steps + i, 0)),
                  pl.BlockSpec((tb, cfg.d_dsa), lambda b, i: (b * n_steps + i, qs_blk)),
                  pl.BlockSpec((t, LANES), lambda b, i: (b, 0)),
                  pl.BlockSpec((t, kvw), lambda b, i: (b, ks_blk)),
                  pl.BlockSpec((t, kvw), lambda b, i: (b, vs_blk))],
        out_specs=pl.BlockSpec((tb, cfg.d_dsa), lambda b, i: (b * n_steps + i, 0)),
        out_shape=jax.ShapeDtypeStruct((m, cfg.d_dsa), BF16),
        scratch_shapes=[pltpu.VMEM((t, LANES), BF16), pltpu.VMEM((t, kvw), BF16), pltpu.VMEM((t, kvw), BF16),
                        pltpu.VMEM((tb, t), F32), pltpu.VMEM((tb, t), F32)],
        compiler_params=_params(("arbitrary", "arbitrary"), 56),
        name="dsa_prompt",
    )(p_p, kiwi_p, p_p, kiwi_p, p_p, p_p)


def _mix_out_kernel(x_ref, hx_ref, gb_ref, gc_ref, ha_ref, hb_ref, cw_ref, yd_ref, ys_ref, z_ref,
                    mod_ref, wo_ref, o_ref, ut_ref, mix_sc, *, tm, seq_tiles, d_conv, d_diff, cols_outer):
    i = pl.program_id(1 if cols_outer else 0)
    j = pl.program_id(0 if cols_outer else 1)
    slot = i if cols_outer else 0
    if cols_outer:
        ut_ref[...] = (gc_ref[...] * hx_ref[...])[tm - SUBLANES:tm]

    @pl.when(j == 0)
    def _():
        u = gc_ref[...] * hx_ref[...]
        uh = ha_ref[...] * hb_ref[...]
        if seq_tiles:
            uh = jnp.where(i % seq_tiles == 0, 0.0, uh)
        row = lax.broadcasted_iota(jnp.int32, (tm, d_conv), 0)
        u1 = jnp.where(row == 0, uh[7:8], pltpu.roll(u, 1, 0))
        u2 = jnp.where(row == 0, uh[6:7], jnp.where(row == 1, uh[7:8], pltpu.roll(u, 2, 0)))
        cw = cw_ref[...]
        yc = gb_ref[...] * (cw[0:1] * u2 + cw[1:2] * u1 + cw[2:3] * u)
        if not cols_outer:
            ut_ref[...] = u[tm - SUBLANES:tm]

        def gated(y, c0, c1):
            zf = z_ref[:, c0:c1].astype(F32)
            mix_sc[slot, :, c0:c1] = (y * (zf * jax.nn.sigmoid(zf))).astype(mix_sc.dtype)

        gated(yc, 0, d_conv)
        gated(yd_ref[...].astype(F32), d_conv, d_conv + d_diff)
        gated(ys_ref[...].astype(F32), d_conv + d_diff, mix_sc.shape[2])

    mix = mix_sc[slot].astype(BF16)
    o_ref[...] = x_ref[...] + mod_ref[2:3, :] * jnp.dot(mix, wo_ref[...], preferred_element_type=F32)


def _mix_out_call(x2d, p_x, halo_a, halo_b, halo_map_a, halo_map_b, conv_w, y_diff, y_dsa, z_x, mod3,
                  wo16, layer, cfg, rows_per_batch, tm, seq_tiles, cols_outer, tn=512):
    m, d = x2d.shape
    dc = cfg.d_conv
    per = rows_per_batch // tm
    hx_blk, gb_blk, gc_blk = (cfg.off[n] // dc for n in ("hx", "gb", "gc"))
    kern = functools.partial(_mix_out_kernel, tm=tm, seq_tiles=seq_tiles, d_conv=dc, d_diff=cfg.d_diff,
                             cols_outer=cols_outer)

    def rc(index_map):
        return (lambda a, b: index_map(b, a)) if cols_outer else index_map

    mix_shape = (m // tm, tm, d) if cols_outer else (1, tm, d)
    return pl.pallas_call(
        kern,
        grid=(d // tn, m // tm) if cols_outer else (m // tm, d // tn),
        in_specs=[pl.BlockSpec((tm, tn), rc(lambda i, j: (i, j))),
                  pl.BlockSpec((tm, dc), rc(lambda i, j: (i, hx_blk))),
                  pl.BlockSpec((tm, dc), rc(lambda i, j: (i, gb_blk))),
                  pl.BlockSpec((tm, dc), rc(lambda i, j: (i, gc_blk))),
                  pl.BlockSpec((SUBLANES, dc), rc(halo_map_a)),
                  pl.BlockSpec((SUBLANES, dc), rc(halo_map_b)),
                  pl.BlockSpec(conv_w.shape, rc(lambda i, j: (0, 0))),
                  pl.BlockSpec((tm, cfg.d_diff), rc(lambda i, j: (i, 0))),
                  pl.BlockSpec((tm, cfg.d_dsa), rc(lambda i, j: (i, 0))),
                  pl.BlockSpec((tm, d), rc(lambda i, j: (i, 0))),
                  pl.BlockSpec((None, 3, tn), rc(lambda i, j: (i // per, 0, j))),
                  pl.BlockSpec((None, d, tn), rc(lambda i, j: (layer, 0, j)))],
        out_specs=[pl.BlockSpec((tm, tn), rc(lambda i, j: (i, j))),
                   pl.BlockSpec((None, None, SUBLANES, dc), rc(lambda i, j: (j if cols_outer else 0, i, 0, 0)))],
        out_shape=[jax.ShapeDtypeStruct((m, d), F32),
                   jax.ShapeDtypeStruct((d // tn if cols_outer else 1, m // tm, SUBLANES, dc), F32)],
        scratch_shapes=[pltpu.VMEM(mix_shape, F32 if cols_outer else BF16)],
        compiler_params=_params(("arbitrary", "arbitrary"), 56),
        name="mix_out",
    )(x2d, p_x, p_x, p_x, halo_a, halo_b, conv_w, y_diff, y_dsa, z_x, mod3, wo16)


ATTN_PAGES = 16
INDEX_PAGES = 32


def _page_specs(layer, pps, rows, width):
    def spec(p):
        return pl.BlockSpec((None, None, rows, width),
                            lambda b, s, pt: (layer, pt[b, s * pps + p], 0, 0))
    return [spec(p) for p in range(pps)]


def _cat_pages(refs, g, n_kv):
    return jnp.concatenate([r[pl.ds(g, PAGE_SIZE, stride=n_kv), :] for r in refs], axis=0).astype(BF16)


def _dec_diff_kernel(pt_ref, q_ref, *refs, pps, n_kv, rep, t_new, lam_init):
    kp, vp = refs[:pps], refs[pps:2 * pps]
    knew_ref, vnew_ref, lam_ref, sg_ref, o_ref, qs_sc, m_sc, l_sc, acc_sc = refs[2 * pps:]
    step = pl.program_id(1)
    rg = 2 * rep * SUBLANES
    groups = [slice(g * rg, (g + 1) * rg) for g in range(n_kv)]

    @pl.when(step == 0)
    def _():
        for h in range(n_kv * rep):
            qh = q_ref[:, h * HEAD_DIM:(h + 1) * HEAD_DIM] * ((HEAD_DIM // 2) ** -0.5)
            lo, hi = _half_masked(qh, SUBLANES)
            qs_sc[2 * h * SUBLANES:(2 * h + 1) * SUBLANES] = lo
            qs_sc[(2 * h + 1) * SUBLANES:(2 * h + 2) * SUBLANES] = hi
        _init_flash(m_sc, l_sc, acc_sc)

    scores = [_dot_nt(qs_sc[rows].astype(BF16), _cat_pages(kp, g, n_kv)) for g, rows in enumerate(groups)]
    for g, rows in enumerate(groups):
        _online_update(m_sc, l_sc, acc_sc, rows, scores[g], _cat_pages(vp, g, n_kv), False)

    @pl.when(step == pl.num_programs(1) - 1)
    def _():
        tok = lax.broadcasted_iota(jnp.int32, (rg, LANES), 0) % SUBLANES
        col = lax.broadcasted_iota(jnp.int32, (rg, LANES), 1)
        valid = (col <= tok) & (col < t_new)
        for g, rows in enumerate(groups):
            c0, c1 = g * HEAD_DIM, (g + 1) * HEAD_DIM
            s = jnp.where(valid, _dot_nt(qs_sc[rows].astype(BF16), knew_ref[:, c0:c1]), NEG_INF)
            _online_update(m_sc, l_sc, acc_sc, rows, s, vnew_ref[:, c0:c1], False)
        lmb = _lambda_full(lam_ref, lam_init)
        o_all = acc_sc[...] / l_sc[...]
        for h in range(n_kv * rep):
            o0 = o_all[2 * h * SUBLANES:(2 * h + 1) * SUBLANES]
            o1 = o_all[(2 * h + 1) * SUBLANES:(2 * h + 2) * SUBLANES]
            o_ref[:, h * HEAD_DIM:(h + 1) * HEAD_DIM] = _diff_head_out(o0, o1, lmb, sg_ref[...], lam_init)


def _flash_scratch(rows):
    return [pltpu.VMEM((rows, HEAD_DIM), F32), pltpu.VMEM((rows, LANES), F32),
            pltpu.VMEM((rows, LANES), F32), pltpu.VMEM((rows, HEAD_DIM), F32)]


def _dec_diff_call(page_table, p_s, cache_k, cache_v, knew16, vnew16, lam, sg, cfg, layer, t_new, lam_init):
    nb, n_pages = page_table.shape
    pps = math.gcd(n_pages, ATTN_PAGES)
    kvw = cfg.n_kv_diff * HEAD_DIM
    rows = cfg.n_kv_diff * 2 * cfg.rep_diff * SUBLANES
    q_blk = cfg.off["q_d"] // cfg.d_diff
    kern = functools.partial(_dec_diff_kernel, pps=pps, n_kv=cfg.n_kv_diff, rep=cfg.rep_diff,
                             t_new=t_new, lam_init=lam_init)
    grid_spec = pltpu.PrefetchScalarGridSpec(
        num_scalar_prefetch=1,
        grid=(nb, n_pages // pps),
        in_specs=[pl.BlockSpec((SUBLANES, cfg.d_diff), lambda b, s, pt: (b, q_blk))]
        + 2 * _page_specs(layer, pps, PAGE_SIZE * cfg.n_kv_diff, HEAD_DIM)
        + [pl.BlockSpec((None, PAGE_SIZE, kvw), lambda b, s, pt: (b, 0, 0)),
           pl.BlockSpec((None, PAGE_SIZE, kvw), lambda b, s, pt: (b, 0, 0)),
           pl.BlockSpec(lam.shape, lambda b, s, pt: (0, 0)),
           pl.BlockSpec((1, HEAD_DIM), lambda b, s, pt: (0, 0))],
        out_specs=pl.BlockSpec((SUBLANES, cfg.d_diff), lambda b, s, pt: (b, 0)),
        scratch_shapes=_flash_scratch(rows))
    return pl.pallas_call(
        kern,
        grid_spec=grid_spec,
        out_shape=jax.ShapeDtypeStruct((nb * SUBLANES, cfg.d_diff), F32),
        compiler_params=_params(("arbitrary", "arbitrary"), 48),
        name="diff_attn_sample",
    )(page_table, p_s, *([cache_k] * pps), *([cache_v] * pps), knew16, vnew16, lam, sg.reshape(1, HEAD_DIM))


def _dec_index_kernel(pt_ref, q_ref, w_ref, *refs, pps, t_new, past_len):
    kp = refs[:pps]
    knew_ref, sc_ref = refs[pps:]
    step = pl.program_id(1)
    span = pps * PAGE_SIZE

    def idx_score(s):
        s = jnp.maximum(s, 0.0) * (w_ref[...] * (N_IDX_HEADS ** -0.5))
        return jnp.sum(s.reshape(N_IDX_HEADS, SUBLANES, s.shape[-1]), axis=0)

    k_t = jnp.concatenate([r[...] for r in kp], axis=1).astype(BF16)
    off = pl.multiple_of(step * span, span)
    sc_ref[:, pl.ds(off, span)] = idx_score(jnp.dot(q_ref[...], k_t, preferred_element_type=F32))

    @pl.when(step == pl.num_programs(1) - 1)
    def _():
        tok = lax.broadcasted_iota(jnp.int32, (SUBLANES, LANES), 0)
        col = lax.broadcasted_iota(jnp.int32, (SUBLANES, LANES), 1)
        valid = (col <= tok) & (col < t_new)
        sc_ref[:, past_len:past_len + LANES] = jnp.where(valid, idx_score(_dot_nt(q_ref[...], knew_ref[...])), NEG_INF)


def _select_kernel(s_ref, o_ref, *, k_top):
    o_ref[...] = _selection_bias(s_ref[...], k_top)


def _select_call(scores, k_top):
    r, w = scores.shape
    return pl.pallas_call(
        functools.partial(_select_kernel, k_top=k_top),
        grid=(1,),
        in_specs=[pl.BlockSpec((r, w), lambda i: (0, 0))],
        out_specs=pl.BlockSpec((r, w), lambda i: (0, 0)),
        out_shape=jax.ShapeDtypeStruct((r, w), F32),
        compiler_params=_params(("arbitrary",), 56),
        name="dsa_select_sample",
    )(scores)


def _dec_index_call(page_table, qi_rows, wi_col, cache_kidx_t, kinew16, layer, t_new):
    nb, n_pages = page_table.shape
    pps = math.gcd(n_pages, INDEX_PAGES)
    past_len = n_pages * PAGE_SIZE
    k_top = min(TOPK_MAX, (past_len + t_new) // 4)
    w = past_len + LANES
    rows = N_IDX_HEADS * SUBLANES
    kern = functools.partial(_dec_index_kernel, pps=pps, t_new=t_new, past_len=past_len)
    grid_spec = pltpu.PrefetchScalarGridSpec(
        num_scalar_prefetch=1,
        grid=(nb, n_pages // pps),
        in_specs=[pl.BlockSpec((None, rows, D_IDX), lambda b, s, pt: (b, 0, 0)),
                  pl.BlockSpec((None, rows, 1), lambda b, s, pt: (b, 0, 0))]
        + _page_specs(layer, pps, D_IDX, PAGE_SIZE)
        + [pl.BlockSpec((None, PAGE_SIZE, D_IDX), lambda b, s, pt: (b, 0, 0))],
        out_specs=pl.BlockSpec((None, SUBLANES, w), lambda b, s, pt: (b, 0, 0)))
    scores = pl.pallas_call(
        kern,
        grid_spec=grid_spec,
        out_shape=jax.ShapeDtypeStruct((nb, SUBLANES, w), F32),
        compiler_params=_params(("arbitrary", "arbitrary"), 40),
        name="dsa_index_sample",
    )(page_table, qi_rows, wi_col, *([cache_kidx_t] * pps), kinew16)
    return _select_call(scores.reshape(nb * SUBLANES, w), k_top).reshape(nb, SUBLANES, w)


def _dec_dsa_kernel(pt_ref, q_ref, bias_ref, bnew_ref, *refs, pps, n_kv, rep):
    kp, vp = refs[:pps], refs[pps:2 * pps]
    knew_ref, vnew_ref, o_ref, qs_sc, m_sc, l_sc, acc_sc = refs[2 * pps:]
    step = pl.program_id(1)
    rg = rep * SUBLANES
    groups = [slice(g * rg, (g + 1) * rg) for g in range(n_kv)]
    scale = HEAD_DIM ** -0.5

    @pl.when(step == 0)
    def _():
        for h in range(n_kv * rep):
            qs_sc[h * SUBLANES:(h + 1) * SUBLANES] = q_ref[:, h * HEAD_DIM:(h + 1) * HEAD_DIM]
        _init_flash(m_sc, l_sc, acc_sc)

    def scores(g, k16, bias):
        s = _dot_nt(qs_sc[groups[g]].astype(BF16), k16) * scale
        n = s.shape[-1]
        return (s.reshape(rep, SUBLANES, n) + bias[None]).reshape(rg, n)

    s_all = [scores(g, _cat_pages(kp, g, n_kv), bias_ref[...]) for g in range(n_kv)]
    for g in range(n_kv):
        _online_update(m_sc, l_sc, acc_sc, groups[g], s_all[g], _cat_pages(vp, g, n_kv), True)

    @pl.when(step == pl.num_programs(1) - 1)
    def _():
        for g in range(n_kv):
            c0, c1 = g * HEAD_DIM, (g + 1) * HEAD_DIM
            _online_update(m_sc, l_sc, acc_sc, groups[g], scores(g, knew_ref[:, c0:c1], bnew_ref[...]),
                           vnew_ref[:, c0:c1], True)
        o_all = acc_sc[...] / l_sc[...]
        for h in range(n_kv * rep):
            o_ref[:, h * HEAD_DIM:(h + 1) * HEAD_DIM] = o_all[h * SUBLANES:(h + 1) * SUBLANES]


def _dec_dsa_call(page_table, p_s, bias, cache_k, cache_v, knew16, vnew16, cfg, layer):
    nb, n_pages = page_table.shape
    pps = math.gcd(n_pages, ATTN_PAGES)
    kvw = cfg.n_kv_dsa * HEAD_DIM
    rows = cfg.n_kv_dsa * cfg.rep_dsa * SUBLANES
    span = pps * PAGE_SIZE
    new_blk = n_pages * PAGE_SIZE // LANES
    q_blk = cfg.off["q_s"] // cfg.d_dsa
    kern = functools.partial(_dec_dsa_kernel, pps=pps, n_kv=cfg.n_kv_dsa, rep=cfg.rep_dsa)
    grid_spec = pltpu.PrefetchScalarGridSpec(
        num_scalar_prefetch=1,
        grid=(nb, n_pages // pps),
        in_specs=[pl.BlockSpec((SUBLANES, cfg.d_dsa), lambda b, s, pt: (b, q_blk)),
                  pl.BlockSpec((None, SUBLANES, span), lambda b, s, pt: (b, 0, s)),
                  pl.BlockSpec((None, SUBLANES, LANES), lambda b, s, pt: (b, 0, new_blk))]
        + 2 * _page_specs(layer, pps, PAGE_SIZE * cfg.n_kv_dsa, HEAD_DIM)
        + [pl.BlockSpec((None, PAGE_SIZE, kvw), lambda b, s, pt: (b, 0, 0)),
           pl.BlockSpec((None, PAGE_SIZE, kvw), lambda b, s, pt: (b, 0, 0))],
        out_specs=pl.BlockSpec((SUBLANES, cfg.d_dsa), lambda b, s, pt: (b, 0)),
        scratch_shapes=_flash_scratch(rows))
    return pl.pallas_call(
        kern,
        grid_spec=grid_spec,
        out_shape=jax.ShapeDtypeStruct((nb * SUBLANES, cfg.d_dsa), F32),
        compiler_params=_params(("arbitrary", "arbitrary"), 40),
        name="dsa_attn_sample",
    )(page_table, p_s, bias, bias, *([cache_k] * pps), *([cache_v] * pps), knew16, vnew16)


def _pad_rows(a, rows):
    return jnp.pad(a, ((0, 0), (0, rows - a.shape[1]), (0, 0)))


def kernel(x_prompt, x_sample, cache_diff_k, cache_diff_v, cache_dsa_k, cache_dsa_v, cache_dsa_kidx,
           state_conv, page_table, c_prompt, c_sample, norm_g, w_ada, b_ada, w_in, conv_w,
           diff_lam, diff_subln_g, w_out, final_g):
    nbp, t, d = x_prompt.shape
    nbs, t_new, _ = x_sample.shape
    depth, n_pool = cache_diff_k.shape[:2]
    cfg = _make_cfg(d, cache_diff_k.shape[3], cache_dsa_k.shape[3])
    dc = cfg.d_conv
    off = cfg.off
    kdw, ksw = cfg.n_kv_diff * HEAD_DIM, cfg.n_kv_dsa * HEAD_DIM
    kiwi_w = D_IDX + N_IDX_HEADS
    assert cache_diff_k.shape[2] == PAGE_SIZE and cache_diff_k.shape[4] == HEAD_DIM
    assert state_conv.shape[2] == CONV_WIDTH - 1 and CONV_WIDTH - 1 <= t_new <= SUBLANES
    assert w_in.shape[2] == cfg.n_main + kiwi_w + d and cfg.d_diff == cfg.d_dsa

    w_main16 = w_in.astype(BF16)
    w_kiwi16 = jnp.pad(w_in[:, :, cfg.n_main:cfg.n_main + kiwi_w], ((0, 0), (0, 0), (0, LANES - kiwi_w))).astype(BF16)
    w_z16 = w_in[:, :, cfg.n_main + kiwi_w:].astype(BF16)
    wo16 = w_out.astype(BF16)
    z_perm = tuple(range(d // PROJ_TN))

    n_c = nbp + nbs
    c_all = _pad_rows(jnp.concatenate([c_prompt, c_sample], axis=0)[None], -(-n_c // SUBLANES) * SUBLANES)[0]
    mod = _mod_call(c_all, w_ada, b_ada).reshape(depth, c_all.shape[0], 3, d)

    ck_d = cache_diff_k.reshape(depth, n_pool, PAGE_SIZE * cfg.n_kv_diff, HEAD_DIM)
    cv_d = cache_diff_v.reshape(depth, n_pool, PAGE_SIZE * cfg.n_kv_diff, HEAD_DIM)
    ck_s = cache_dsa_k.reshape(depth, n_pool, PAGE_SIZE * cfg.n_kv_dsa, HEAD_DIM)
    cv_s = cache_dsa_v.reshape(depth, n_pool, PAGE_SIZE * cfg.n_kv_dsa, HEAD_DIM)
    ck_i = jnp.swapaxes(cache_dsa_kidx, 2, 3)

    xp = x_prompt.reshape(nbp * t, d)
    xs = _pad_rows(x_sample, SUBLANES).reshape(nbs * SUBLANES, d)
    ones_halo = jnp.ones((SUBLANES, dc), F32)
    tm_p = 512
    seq_tiles = t // tm_p
    halo_blocks = tm_p // SUBLANES
    ms = nbs * SUBLANES

    def prompt_halo(name):
        col_blk = off[name] // dc
        return lambda i, j: (jnp.maximum(i * halo_blocks - 1, 0), col_blk)

    def project(h16, l, tm, z_dtype):
        p = _matmul_call(h16, w_main16, l, cfg.perm, F32, tm, PROJ_TN)
        z = _matmul_call(h16, w_z16, l, z_perm, z_dtype, tm, PROJ_TN)
        kiwi = _matmul_call(h16, w_kiwi16, l, (0,), F32, tm, LANES)
        return p, z, kiwi

    p_rows, s_rows = [], []
    for l in range(depth):
        lam_init = LAMBDA_INIT_BASE - LAMBDA_INIT_AMP * math.exp(-LAMBDA_INIT_RATE * l)
        mod_p, mod_s = mod[l, :nbp], mod[l, nbp:n_c]

        h_p = _norm_mod_call(xp, norm_g[l], mod_p, t, 256, BF16)
        p_p, z_p, kiwi_p = project(h_p, l, 1024, BF16)
        yd_p = _diff_prompt_call(p_p, diff_lam[l], diff_subln_g[l], cfg, nbp, t, lam_init)
        ys_p = _dsa_prompt_call(p_p, kiwi_p, cfg, nbp, t)
        xp, ut_p = _mix_out_call(xp, p_p, p_p, p_p, prompt_halo("hx"), prompt_halo("gc"), conv_w[l], yd_p, ys_p,
                                 z_p, mod_p, wo16, l, cfg, t, tm_p, seq_tiles, False)
        p_rows.append((p_p, kiwi_p,
                       ut_p.reshape(nbp, seq_tiles, SUBLANES, dc)[:, -1, SUBLANES - (CONV_WIDTH - 1):]))

        h_s = _norm_mod_call(xs, norm_g[l], mod_s, SUBLANES, SUBLANES, F32).astype(BF16)
        p_s, z_s, kiwi_s = project(h_s, l, ms, F32)
        p_s3 = p_s.reshape(nbs, SUBLANES, cfg.n_main)
        kiwi_s3 = kiwi_s.reshape(nbs, SUBLANES, LANES)

        def new_rows(name, width):
            return _pad_rows(p_s3[:, :, off[name]:off[name] + width], PAGE_SIZE).astype(BF16)

        yd_s = _dec_diff_call(page_table, p_s, ck_d, cv_d, new_rows("k_d", kdw), new_rows("v_d", kdw),
                              diff_lam[l], diff_subln_g[l], cfg, l, t_new, lam_init)
        qi_rows = p_s3[:, :, off["qi"]:off["qi"] + N_IDX_HEADS * D_IDX].reshape(nbs, SUBLANES, N_IDX_HEADS, D_IDX)
        qi_rows = qi_rows.transpose(0, 2, 1, 3).reshape(nbs, N_IDX_HEADS * SUBLANES, D_IDX).astype(BF16)
        wi_col = kiwi_s3[:, :, D_IDX:kiwi_w].transpose(0, 2, 1).reshape(nbs, N_IDX_HEADS * SUBLANES, 1)
        ki_new = _pad_rows(kiwi_s3[:, :, :D_IDX], PAGE_SIZE).astype(BF16)
        bias = _dec_index_call(page_table, qi_rows, wi_col, ck_i, ki_new, l, t_new)
        ys_s = _dec_dsa_call(page_table, p_s, bias, ck_s, cv_s, new_rows("k_s", ksw), new_rows("v_s", ksw),
                             cfg, l)
        state_pad = jnp.pad(state_conv[l], ((0, 0), (SUBLANES - (CONV_WIDTH - 1), 0), (0, 0)))
        xs, ut_s = _mix_out_call(xs, p_s, state_pad.reshape(ms, dc), ones_halo,
                                 lambda i, j: (i, 0), lambda i, j: (0, 0), conv_w[l], yd_s, ys_s,
                                 z_s, mod_s, wo16, l, cfg, SUBLANES, SUBLANES, 0, True)
        s_rows.append((p_s3[:, :t_new], kiwi_s3[:, :t_new], ut_s[0, :, t_new - (CONV_WIDTH - 1):t_new]))

    y_prompt = _rmsnorm_call(xp, final_g, 256).reshape(nbp, t, d)
    y_sample = _rmsnorm_call(xs, final_g, SUBLANES).reshape(nbs, SUBLANES, d)[:, :t_new]

    def stack_p(name, n_kv):
        return jnp.stack([r[0][:, off[name]:off[name] + n_kv * HEAD_DIM].reshape(nbp, t, n_kv, HEAD_DIM)
                          for r in p_rows])

    def stack_s(name, n_kv):
        return jnp.stack([r[0][:, :, off[name]:off[name] + n_kv * HEAD_DIM].reshape(nbs, t_new, n_kv, HEAD_DIM)
                          for r in s_rows])

    return (y_prompt, y_sample,
            stack_p("k_d", cfg.n_kv_diff), stack_p("v_d", cfg.n_kv_diff),
            stack_p("k_s", cfg.n_kv_dsa), stack_p("v_s", cfg.n_kv_dsa),
            jnp.stack([r[1][:, :D_IDX].reshape(nbp, t, D_IDX) for r in p_rows]),
            jnp.stack([r[2] for r in p_rows]),
            stack_s("k_d", cfg.n_kv_diff), stack_s("v_d", cfg.n_kv_diff),
            stack_s("k_s", cfg.n_kv_dsa), stack_s("v_s", cfg.n_kv_dsa),
            jnp.stack([r[1][:, :, :D_IDX] for r in s_rows]),
            jnp.stack([r[2] for r in s_rows]))
```

```python
import functools
import math
from typing import NamedTuple

import jax
import jax.numpy as jnp
from jax import lax
from jax.experimental import pallas as pl
from jax.experimental.pallas import tpu as pltpu

F32 = jnp.float32
BF16 = jnp.bfloat16

HEAD_DIM = 128
D_IDX = 64
N_IDX_HEADS = 16
CONV_WIDTH = 3
TOPK_MAX = 256
PAGE_SIZE = 128
RMS_EPS = 1e-6
LAMBDA_INIT_BASE = 0.8
LAMBDA_INIT_AMP = 0.6
LAMBDA_INIT_RATE = 0.3

LANES = 128
SUBLANES = 8
MIB = 2 ** 20
PROJ_TN = 512

NEG_INF = float("-inf")
INT_MIN = -(2 ** 31)
KEY_NEG_INF = INT_MIN + 0x7FFFFF


class Cfg(NamedTuple):
    d_model: int
    d_conv: int
    n_kv_diff: int
    rep_diff: int
    d_diff: int
    n_kv_dsa: int
    rep_dsa: int
    d_dsa: int
    n_main: int
    off: dict
    perm: tuple


def _make_cfg(d_model, n_kv_diff, n_kv_dsa):
    d_conv = d_model // 4
    d_diff = 3 * d_model // 8
    d_dsa = 3 * d_model // 8
    kd, ks = n_kv_diff * HEAD_DIM, n_kv_dsa * HEAD_DIM
    qi = N_IDX_HEADS * D_IDX
    w_order = (("hx", d_conv), ("gb", d_conv), ("gc", d_conv), ("q_d", d_diff), ("k_d", kd), ("v_d", kd),
               ("q_s", d_dsa), ("kv_s", 2 * ks), ("qi", qi))
    out_order = ("q_d", "q_s", "qi", "hx", "gb", "gc", "k_d", "v_d", "kv_s")
    width = dict(w_order)
    off, pos = {}, 0
    for name in out_order:
        assert width[name] % PROJ_TN == 0
        off[name] = pos
        pos += width[name]
    off["k_s"], off["v_s"] = off["kv_s"], off["kv_s"] + ks
    perm = []
    for name, wd in w_order:
        perm.extend(off[name] // PROJ_TN + b for b in range(wd // PROJ_TN))
    return Cfg(d_model, d_conv, n_kv_diff, d_diff // HEAD_DIM // n_kv_diff, d_diff,
               n_kv_dsa, d_dsa // HEAD_DIM // n_kv_dsa, d_dsa, pos, off, tuple(perm))


def _params(semantics, vmem_mib):
    return pltpu.CompilerParams(dimension_semantics=semantics, vmem_limit_bytes=vmem_mib * MIB)


def _dot_nt(a, b):
    return lax.dot_general(a, b, (((1,), (1,)), ((), ())), preferred_element_type=F32)


MOD_STREAMS = 8
MOD_ROWS = 32


def _mod_kernel(c_ref, *refs):
    w_refs, (b_ref, o_ref) = refs[:MOD_STREAMS], refs[MOD_STREAMS:]

    @pl.when(pl.program_id(1) == 0)
    def _():
        o_ref[...] = jnp.broadcast_to(b_ref[...], o_ref.shape)

    c = c_ref[...]
    a = (c * jax.nn.sigmoid(c)).astype(BF16)
    w = jnp.concatenate([r[...].astype(BF16) for r in w_refs], axis=0)
    o_ref[...] += jnp.dot(a, w, preferred_element_type=F32)


def _mod_call(c_all, w_ada, b_ada):
    depth, d, n3 = w_ada.shape
    rows = c_all.shape[0]
    tk = MOD_STREAMS * MOD_ROWS

    def w_spec(p):
        return pl.BlockSpec((None, MOD_ROWS, n3), lambda l, k: (l, k * MOD_STREAMS + p, 0))

    return pl.pallas_call(
        _mod_kernel,
        grid=(depth, d // tk),
        in_specs=[pl.BlockSpec((rows, tk), lambda l, k: (0, k))]
        + [w_spec(p) for p in range(MOD_STREAMS)]
        + [pl.BlockSpec((None, 1, n3), lambda l, k: (l, 0, 0))],
        out_specs=pl.BlockSpec((None, rows, n3), lambda l, k: (l, 0, 0)),
        out_shape=jax.ShapeDtypeStruct((depth, rows, n3), F32),
        compiler_params=_params(("arbitrary", "arbitrary"), 48),
        name="adaln_mod",
    )(c_all, *([w_ada] * MOD_STREAMS), b_ada.reshape(depth, 1, n3))


def _rms(x):
    return x * lax.rsqrt(jnp.mean(x * x, axis=-1, keepdims=True) + RMS_EPS)


NORM_STREAMS = 4


def _row_streams(tm):
    return NORM_STREAMS if tm % (NORM_STREAMS * SUBLANES) == 0 else 1


def _row_stream_specs(tm, d, streams):
    rows = tm // streams

    def spec(p):
        return pl.BlockSpec((rows, d), lambda i: (i * streams + p, 0))
    return [spec(p) for p in range(streams)]


def _norm_mod_kernel(*refs):
    x_refs, (g_ref, mod_ref, o_ref) = refs[:-3], refs[-3:]
    x = jnp.concatenate([r[...] for r in x_refs], axis=0)
    y = _rms(x) * g_ref[...]
    o_ref[...] = (y * (1.0 + mod_ref[1:2, :]) + mod_ref[0:1, :]).astype(o_ref.dtype)


def _norm_mod_call(x2d, g, mod3, rows_per_batch, tm, out_dtype):
    m, d = x2d.shape
    per = rows_per_batch // tm
    streams = _row_streams(tm)
    return pl.pallas_call(
        _norm_mod_kernel,
        grid=(m // tm,),
        in_specs=_row_stream_specs(tm, d, streams)
        + [pl.BlockSpec((1, d), lambda i: (0, 0)),
           pl.BlockSpec((None, 3, d), lambda i: (i // per, 0, 0))],
        out_specs=pl.BlockSpec((tm, d), lambda i: (i, 0)),
        out_shape=jax.ShapeDtypeStruct((m, d), out_dtype),
        compiler_params=_params(("arbitrary",), 40),
        name="norm_mod",
    )(*([x2d] * streams), g.reshape(1, d), mod3)


def _rmsnorm_kernel(*refs):
    x_refs, (g_ref, o_ref) = refs[:-2], refs[-2:]
    o_ref[...] = _rms(jnp.concatenate([r[...] for r in x_refs], axis=0)) * g_ref[...]


def _rmsnorm_call(x2d, g, tm):
    m, d = x2d.shape
    streams = _row_streams(tm)
    return pl.pallas_call(
        _rmsnorm_kernel,
        grid=(m // tm,),
        in_specs=_row_stream_specs(tm, d, streams) + [pl.BlockSpec((1, d), lambda i: (0, 0))],
        out_specs=pl.BlockSpec((tm, d), lambda i: (i, 0)),
        out_shape=jax.ShapeDtypeStruct((m, d), F32),
        compiler_params=_params(("arbitrary",), 40),
        name="final_norm",
    )(*([x2d] * streams), g.reshape(1, d))


def _matmul_kernel(perm_ref, x_ref, w_ref, o_ref):
    o_ref[...] = _dot_nt(x_ref[...], w_ref[...]).astype(o_ref.dtype)


def _matmul_call(x16, wt16, layer, perm, out_dtype, tm, tn):
    m, k = x16.shape
    n_blocks = len(perm)
    grid_spec = pltpu.PrefetchScalarGridSpec(
        num_scalar_prefetch=1,
        grid=(m // tm, n_blocks),
        in_specs=[pl.BlockSpec((tm, k), lambda i, j, pm: (i, 0)),
                  pl.BlockSpec((None, tn, k), lambda i, j, pm: (layer, j, 0))],
        out_specs=pl.BlockSpec((tm, tn), lambda i, j, pm: (i, pm[j])))
    return pl.pallas_call(
        _matmul_kernel,
        grid_spec=grid_spec,
        out_shape=jax.ShapeDtypeStruct((m, n_blocks * tn), out_dtype),
        compiler_params=_params(("arbitrary", "arbitrary"), 48),
        name="in_proj",
    )(jnp.asarray(perm, jnp.int32), x16, wt16)


def _half_masked(q, rows):
    lane = lax.broadcasted_iota(jnp.int32, (rows, HEAD_DIM), 1)
    zero = jnp.zeros_like(q)
    return jnp.where(lane < HEAD_DIM // 2, q, zero), jnp.where(lane >= HEAD_DIM // 2, q, zero)


def _lambda_full(lam_ref, lam_init):
    lam = lam_ref[...]
    a = jnp.sum(lam[0:1, :] * lam[1:2, :], axis=-1, keepdims=True)
    b = jnp.sum(lam[2:3, :] * lam[3:4, :], axis=-1, keepdims=True)
    return jnp.exp(a) - jnp.exp(b) + lam_init


def _diff_head_out(o0, o1, lmb, sg, lam_init):
    o = o0 - lmb * o1
    return _rms(o) * sg * (1.0 - lam_init)


def _online_update(m_sc, l_sc, acc_sc, rows, s, v16, guard_empty):
    m_prev = m_sc[rows]
    m_new = jnp.maximum(m_prev, jnp.max(s, axis=-1, keepdims=True))
    m_use = jnp.where(m_new == NEG_INF, 0.0, m_new) if guard_empty else m_new
    alpha = jnp.exp(m_prev - m_use)
    p = jnp.exp(s - jnp.concatenate([m_use] * (s.shape[-1] // LANES), axis=1))
    l_sc[rows] = alpha * l_sc[rows] + jnp.sum(p, axis=-1, keepdims=True)
    acc_sc[rows] = alpha * acc_sc[rows] + jnp.dot(p.astype(BF16), v16, preferred_element_type=F32)
    m_sc[rows] = m_new


def _init_flash(m_sc, l_sc, acc_sc):
    m_sc[...] = jnp.full_like(m_sc, NEG_INF)
    l_sc[...] = jnp.zeros_like(l_sc)
    acc_sc[...] = jnp.zeros_like(acc_sc)


def _diff_prompt_kernel(q_ref, k_ref, v_ref, lam_ref, sg_ref, o_ref, qs_sc, m_sc, l_sc, acc_sc,
                        *, rep, tq, lam_init):
    i = pl.program_id(2)
    chunk = LANES
    n_chunks = 2 * rep * tq // chunk
    for r in range(rep):
        qr = q_ref[:, r * HEAD_DIM:(r + 1) * HEAD_DIM] * ((HEAD_DIM // 2) ** -0.5)
        lo, hi = _half_masked(qr, tq)
        qs_sc[(2 * r) * tq:(2 * r + 1) * tq] = lo.astype(BF16)
        qs_sc[(2 * r + 1) * tq:(2 * r + 2) * tq] = hi.astype(BF16)
    _init_flash(m_sc, l_sc, acc_sc)

    def kv_block(j, diagonal):
        start = pl.multiple_of(j * tq, tq)
        k = k_ref[pl.ds(start, tq), :].astype(BF16)
        v = v_ref[pl.ds(start, tq), :].astype(BF16)
        for c in range(n_chunks):
            rows = slice(c * chunk, (c + 1) * chunk)
            s = _dot_nt(qs_sc[rows], k)
            if diagonal:
                q0 = (c * chunk) % tq
                row = lax.broadcasted_iota(jnp.int32, (chunk, tq), 0) + q0
                col = lax.broadcasted_iota(jnp.int32, (chunk, tq), 1)
                s = jnp.where(col <= row, s, NEG_INF)
            _online_update(m_sc, l_sc, acc_sc, rows, s, v, False)

    def full_block(j, carry):
        kv_block(j, False)
        return carry

    lax.fori_loop(0, i, full_block, 0)
    kv_block(i, True)

    lmb = _lambda_full(lam_ref, lam_init)
    for r in range(rep):
        lo = slice((2 * r) * tq, (2 * r + 1) * tq)
        hi = slice((2 * r + 1) * tq, (2 * r + 2) * tq)
        y = _diff_head_out(acc_sc[lo] / l_sc[lo], acc_sc[hi] / l_sc[hi], lmb, sg_ref[...], lam_init)
        o_ref[:, r * HEAD_DIM:(r + 1) * HEAD_DIM] = y.astype(BF16)


def _diff_prompt_call(p_p, lam, sg, cfg, nb, t, lam_init, tq=256):
    m = p_p.shape[0]
    nq = t // tq
    g_w = cfg.rep_diff * HEAD_DIM
    rows = 2 * cfg.rep_diff * tq
    q_blk, k_blk, v_blk = cfg.off["q_d"] // g_w, cfg.off["k_d"] // HEAD_DIM, cfg.off["v_d"] // HEAD_DIM
    kern = functools.partial(_diff_prompt_kernel, rep=cfg.rep_diff, tq=tq, lam_init=lam_init)
    return pl.pallas_call(
        kern,
        grid=(nb, cfg.n_kv_diff, nq),
        in_specs=[pl.BlockSpec((tq, g_w), lambda b, g, i: (b * nq + i, q_blk + g)),
                  pl.BlockSpec((t, HEAD_DIM), lambda b, g, i: (b, k_blk + g)),
                  pl.BlockSpec((t, HEAD_DIM), lambda b, g, i: (b, v_blk + g)),
                  pl.BlockSpec(lam.shape, lambda b, g, i: (0, 0)),
                  pl.BlockSpec((1, HEAD_DIM), lambda b, g, i: (0, 0))],
        out_specs=pl.BlockSpec((tq, g_w), lambda b, g, i: (b * nq + i, g)),
        out_shape=jax.ShapeDtypeStruct((m, cfg.d_diff), BF16),
        scratch_shapes=[pltpu.VMEM((rows, HEAD_DIM), BF16), pltpu.VMEM((rows, LANES), F32),
                        pltpu.VMEM((rows, LANES), F32), pltpu.VMEM((rows, HEAD_DIM), F32)],
        compiler_params=_params(("arbitrary", "arbitrary", "arbitrary"), 40),
        name="diff_attn_prompt",
    )(p_p, p_p, p_p, lam, sg.reshape(1, HEAD_DIM))


def _monotone_key(x):
    bits = lax.bitcast_convert_type(x, jnp.int32)
    key = bits ^ ((bits >> 31) & jnp.int32(0x7FFFFFFF))
    return jnp.where(bits == jnp.int32(INT_MIN), 0, key)


def _count(mask):
    return jnp.sum(jnp.where(mask, 1.0, 0.0), axis=-1, keepdims=True)


def _topk_mask(score, k_top):
    r, w = score.shape
    key = _monotone_key(score)
    kf = float(k_top)

    def value_bit(it, t):
        cand = t + (jnp.int32(1) << (31 - it))
        return jnp.where(_count(key >= cand) >= kf, cand, t)

    t = lax.fori_loop(0, 32, value_bit, jnp.full((r, 1), INT_MIN, jnp.int32))
    gt = key > t
    eq = key == t
    n_gt = _count(gt)
    need = kf - n_gt
    col = lax.broadcasted_iota(jnp.int32, (r, w), 1)
    ambiguous = (n_gt + _count(eq) > kf) & (t > KEY_NEG_INF)
    n_bits = w.bit_length()

    def tie_search(_):
        def index_bit(it, j):
            cand = j + (jnp.int32(1) << (n_bits - 1 - it))
            return jnp.where(_count(eq & (col < cand)) < need, cand, j)
        return lax.fori_loop(0, n_bits, index_bit, jnp.zeros((r, 1), jnp.int32))

    any_amb = jnp.max(jnp.where(ambiguous, 1.0, 0.0)) > 0.5
    j = lax.cond(any_amb, tie_search, lambda _: jnp.full((r, 1), w, jnp.int32), 0)
    return gt | (eq & (col <= j))


def _selection_bias(score, k_top):
    sel = _topk_mask(score, k_top) & (score > NEG_INF)
    return jnp.where(sel, 0.0, NEG_INF)


DSA_UNIT = 128
DSA_UNITS = 2


def _dsa_prompt_kernel(qi_ref, wi_ref, qs_ref, ki_ref, ks_ref, vs_ref, o_ref, kiki_sc, ks_sc, vs_sc,
                       score_sc, bias_sc, *, widths, steps_per_width, k_top, n_kv, rep):
    i = pl.program_id(1)

    @pl.when(i == 0)
    def _():
        ki = ki_ref[...]
        lane = lax.broadcasted_iota(jnp.int32, ki.shape, 1)
        kiki_sc[...] = jnp.where(lane < D_IDX, ki, pltpu.roll(ki, D_IDX, 1)).astype(BF16)
        ks_sc[...] = ks_ref[...].astype(BF16)
        vs_sc[...] = vs_ref[...].astype(BF16)

    for c, w in enumerate(widths):
        pl.when(i // steps_per_width == c)(
            functools.partial(_dsa_prompt_body, qi_ref, wi_ref, qs_ref, kiki_sc, ks_sc, vs_sc, o_ref, score_sc,
                              bias_sc, i=i, w=w, k_top=k_top, n_kv=n_kv, rep=rep))


def _dsa_prompt_body(qi_ref, wi_ref, qs_ref, kiki_sc, ks_sc, vs_sc, o_ref, score_sc, bias_sc,
                     *, i, w, k_top, n_kv, rep):
    tq = DSA_UNIT
    first_row = i * (DSA_UNITS * tq)
    kiki = kiki_sc[0:w, :]

    def index_pass(u, carry):
        r0 = pl.multiple_of(u * tq, tq)
        rows = pl.ds(r0, tq)
        wi = wi_ref[rows, :][:, D_IDX:D_IDX + N_IDX_HEADS] * (N_IDX_HEADS ** -0.5)
        idx = jnp.zeros((tq, w), F32)
        for j in range(N_IDX_HEADS // 2):
            lo, hi = _half_masked(qi_ref[rows, j * LANES:(j + 1) * LANES], tq)
            sc = jnp.maximum(_dot_nt(jnp.concatenate([lo, hi], axis=0).astype(BF16), kiki), 0.0)
            idx = idx + sc[:tq] * wi[:, 2 * j:2 * j + 1] + sc[tq:] * wi[:, 2 * j + 1:2 * j + 2]
        row = lax.broadcasted_iota(jnp.int32, (tq, w), 0) + (first_row + r0)
        col = lax.broadcasted_iota(jnp.int32, (tq, w), 1)
        score_sc[rows, 0:w] = jnp.where(col <= row, idx, NEG_INF)
        return carry

    lax.fori_loop(0, DSA_UNITS, index_pass, 0)
    bias_sc[:, 0:w] = _selection_bias(score_sc[:, 0:w], k_top)

    def attend_pass(u, carry):
        rows = pl.ds(pl.multiple_of(u * tq, tq), tq)
        bias = bias_sc[rows, 0:w]
        for g in range(n_kv):
            heads = [qs_ref[rows, (g * rep + r) * HEAD_DIM:(g * rep + r + 1) * HEAD_DIM] for r in range(rep)]
            q = jnp.concatenate(heads, axis=0).astype(BF16)
            s = _dot_nt(q, ks_sc[0:w, g * HEAD_DIM:(g + 1) * HEAD_DIM]) * (HEAD_DIM ** -0.5)
            s = s.reshape(rep, tq, w) + bias[None]
            p = jnp.exp(s - jnp.max(s, axis=-1, keepdims=True))
            l = jnp.sum(p, axis=-1, keepdims=True).reshape(rep * tq, 1)
            o = jnp.dot(p.reshape(rep * tq, w).astype(BF16), vs_sc[0:w, g * HEAD_DIM:(g + 1) * HEAD_DIM],
                        preferred_element_type=F32) / l
            for r in range(rep):
                o_ref[rows, (g * rep + r) * HEAD_DIM:(g * rep + r + 1) * HEAD_DIM] = o[r * tq:(r + 1) * tq].astype(BF16)
        return carry

    lax.fori_loop(0, DSA_UNITS, attend_pass, 0)


def _dsa_prompt_call(p_p, kiwi_p, cfg, nb, t):
    m = p_p.shape[0]
    tb = DSA_UNITS * DSA_UNIT
    n_steps = t // tb
    k_top = min(TOPK_MAX, t // 4)
    qi_w = N_IDX_HEADS * D_IDX
    kvw = cfg.n_kv_dsa * HEAD_DIM
    qi_blk, qs_blk = cfg.off["qi"] // qi_w, cfg.off["q_s"] // cfg.d_dsa
    ks_blk, vs_blk = cfg.off["k_s"] // kvw, cfg.off["v_s"] // kvw
    n_widths = math.gcd(n_steps, 4)
    widths = tuple(t * (c + 1) // n_widths for c in range(n_widths))
    kern = functools.partial(_dsa_prompt_kernel, widths=widths, steps_per_width=n_steps // n_widths, k_top=k_top,
                             n_kv=cfg.n_kv_dsa, rep=cfg.rep_dsa)
    return pl.pallas_call(
        kern,
        grid=(nb, n_steps),
        in_specs=[pl.BlockSpec((tb, qi_w), lambda b, i: (b * n_steps + i, qi_blk)),
                  pl.BlockSpec((tb, LANES), lambda b, i: (b * n_steps + i, 0)),
                  pl.BlockSpec((tb, cfg.d_dsa), lambda b, i: (b * n_steps + i, qs_blk)),
                  pl.BlockSpec((t, LANES), lambda b, i: (b, 0)),
                  pl.BlockSpec((t, kvw), lambda b, i: (b, ks_blk)),
                  pl.BlockSpec((t, kvw), lambda b, i: (b, vs_blk))],
        out_specs=pl.BlockSpec((tb, cfg.d_dsa), lambda b, i: (b * n_steps + i, 0)),
        out_shape=jax.ShapeDtypeStruct((m, cfg.d_dsa), BF16),
        scratch_shapes=[pltpu.VMEM((t, LANES), BF16), pltpu.VMEM((t, kvw), BF16), pltpu.VMEM((t, kvw), BF16),
                        pltpu.VMEM((tb, t), F32), pltpu.VMEM((tb, t), F32)],
        compiler_params=_params(("arbitrary", "arbitrary"), 56),
        name="dsa_prompt",
    )(p_p, kiwi_p, p_p, kiwi_p, p_p, p_p)


def _mix_out_kernel(x_ref, hx_ref, gb_ref, gc_ref, ha_ref, hb_ref, cw_ref, yd_ref, ys_ref, z_ref,
                    mod_ref, wo_ref, o_ref, ut_ref, mix_sc, *, tm, seq_tiles, d_conv, d_diff, cols_outer):
    i = pl.program_id(1 if cols_outer else 0)
    j = pl.program_id(0 if cols_outer else 1)
    slot = i if cols_outer else 0
    if cols_outer:
        ut_ref[...] = (gc_ref[...] * hx_ref[...])[tm - SUBLANES:tm]

    @pl.when(j == 0)
    def _():
        u = gc_ref[...] * hx_ref[...]
        uh = ha_ref[...] * hb_ref[...]
        if seq_tiles:
            uh = jnp.where(i % seq_tiles == 0, 0.0, uh)
        row = lax.broadcasted_iota(jnp.int32, (tm, d_conv), 0)
        u1 = jnp.where(row == 0, uh[7:8], pltpu.roll(u, 1, 0))
        u2 = jnp.where(row == 0, uh[6:7], jnp.where(row == 1, uh[7:8], pltpu.roll(u, 2, 0)))
        cw = cw_ref[...]
        yc = gb_ref[...] * (cw[0:1] * u2 + cw[1:2] * u1 + cw[2:3] * u)
        if not cols_outer:
            ut_ref[...] = u[tm - SUBLANES:tm]

        def gated(y, c0, c1):
            zf = z_ref[:, c0:c1].astype(F32)
            mix_sc[slot, :, c0:c1] = (y * (zf * jax.nn.sigmoid(zf))).astype(mix_sc.dtype)

        gated(yc, 0, d_conv)
        gated(yd_ref[...].astype(F32), d_conv, d_conv + d_diff)
        gated(ys_ref[...].astype(F32), d_conv + d_diff, mix_sc.shape[2])

    mix = mix_sc[slot].astype(BF16)
    o_ref[...] = x_ref[...] + mod_ref[2:3, :] * jnp.dot(mix, wo_ref[...], preferred_element_type=F32)


def _mix_out_call(x2d, p_x, halo_a, halo_b, halo_map_a, halo_map_b, conv_w, y_diff, y_dsa, z_x, mod3,
                  wo16, layer, cfg, rows_per_batch, tm, seq_tiles, cols_outer, tn=512):
    m, d = x2d.shape
    dc = cfg.d_conv
    per = rows_per_batch // tm
    hx_blk, gb_blk, gc_blk = (cfg.off[n] // dc for n in ("hx", "gb", "gc"))
    kern = functools.partial(_mix_out_kernel, tm=tm, seq_tiles=seq_tiles, d_conv=dc, d_diff=cfg.d_diff,
                             cols_outer=cols_outer)

    def rc(index_map):
        return (lambda a, b: index_map(b, a)) if cols_outer else index_map

    mix_shape = (m // tm, tm, d) if cols_outer else (1, tm, d)
    return pl.pallas_call(
        kern,
        grid=(d // tn, m // tm) if cols_outer else (m // tm, d // tn),
        in_specs=[pl.BlockSpec((tm, tn), rc(lambda i, j: (i, j))),
                  pl.BlockSpec((tm, dc), rc(lambda i, j: (i, hx_blk))),
                  pl.BlockSpec((tm, dc), rc(lambda i, j: (i, gb_blk))),
                  pl.BlockSpec((tm, dc), rc(lambda i, j: (i, gc_blk))),
                  pl.BlockSpec((SUBLANES, dc), rc(halo_map_a)),
                  pl.BlockSpec((SUBLANES, dc), rc(halo_map_b)),
                  pl.BlockSpec(conv_w.shape, rc(lambda i, j: (0, 0))),
                  pl.BlockSpec((tm, cfg.d_diff), rc(lambda i, j: (i, 0))),
                  pl.BlockSpec((tm, cfg.d_dsa), rc(lambda i, j: (i, 0))),
                  pl.BlockSpec((tm, d), rc(lambda i, j: (i, 0))),
                  pl.BlockSpec((None, 3, tn), rc(lambda i, j: (i // per, 0, j))),
                  pl.BlockSpec((None, d, tn), rc(lambda i, j: (layer, 0, j)))],
        out_specs=[pl.BlockSpec((tm, tn), rc(lambda i, j: (i, j))),
                   pl.BlockSpec((None, None, SUBLANES, dc), rc(lambda i, j: (j if cols_outer else 0, i, 0, 0)))],
        out_shape=[jax.ShapeDtypeStruct((m, d), F32),
                   jax.ShapeDtypeStruct((d // tn if cols_outer else 1, m // tm, SUBLANES, dc), F32)],
        scratch_shapes=[pltpu.VMEM(mix_shape, F32 if cols_outer else BF16)],
        compiler_params=_params(("arbitrary", "arbitrary"), 56),
        name="mix_out",
    )(x2d, p_x, p_x, p_x, halo_a, halo_b, conv_w, y_diff, y_dsa, z_x, mod3, wo16)


ATTN_PAGES = 16
INDEX_PAGES = 32


def _page_specs(layer, pps, rows, width):
    def spec(p):
        return pl.BlockSpec((None, None, rows, width),
                            lambda b, s, pt: (layer, pt[b, s * pps + p], 0, 0))
    return [spec(p) for p in range(pps)]


def _cat_pages(refs, g, n_kv):
    return jnp.concatenate([r[pl.ds(g, PAGE_SIZE, stride=n_kv), :] for r in refs], axis=0).astype(BF16)


def _dec_diff_kernel(pt_ref, q_ref, *refs, pps, n_kv, rep, t_new, lam_init):
    kp, vp = refs[:pps], refs[pps:2 * pps]
    knew_ref, vnew_ref, lam_ref, sg_ref, o_ref, qs_sc, m_sc, l_sc, acc_sc = refs[2 * pps:]
    step = pl.program_id(1)
    rg = 2 * rep * SUBLANES
    groups = [slice(g * rg, (g + 1) * rg) for g in range(n_kv)]

    @pl.when(step == 0)
    def _():
        for h in range(n_kv * rep):
            qh = q_ref[:, h * HEAD_DIM:(h + 1) * HEAD_DIM] * ((HEAD_DIM // 2) ** -0.5)
            lo, hi = _half_masked(qh, SUBLANES)
            qs_sc[2 * h * SUBLANES:(2 * h + 1) * SUBLANES] = lo
            qs_sc[(2 * h + 1) * SUBLANES:(2 * h + 2) * SUBLANES] = hi
        _init_flash(m_sc, l_sc, acc_sc)

    scores = [_dot_nt(qs_sc[rows].astype(BF16), _cat_pages(kp, g, n_kv)) for g, rows in enumerate(groups)]
    for g, rows in enumerate(groups):
        _online_update(m_sc, l_sc, acc_sc, rows, scores[g], _cat_pages(vp, g, n_kv), False)

    @pl.when(step == pl.num_programs(1) - 1)
    def _():
        tok = lax.broadcasted_iota(jnp.int32, (rg, LANES), 0) % SUBLANES
        col = lax.broadcasted_iota(jnp.int32, (rg, LANES), 1)
        valid = (col <= tok) & (col < t_new)
        for g, rows in enumerate(groups):
            c0, c1 = g * HEAD_DIM, (g + 1) * HEAD_DIM
            s = jnp.where(valid, _dot_nt(qs_sc[rows].astype(BF16), knew_ref[:, c0:c1]), NEG_INF)
            _online_update(m_sc, l_sc, acc_sc, rows, s, vnew_ref[:, c0:c1], False)
        lmb = _lambda_full(lam_ref, lam_init)
        o_all = acc_sc[...] / l_sc[...]
        for h in range(n_kv * rep):
            o0 = o_all[2 * h * SUBLANES:(2 * h + 1) * SUBLANES]
            o1 = o_all[(2 * h + 1) * SUBLANES:(2 * h + 2) * SUBLANES]
            o_ref[:, h * HEAD_DIM:(h + 1) * HEAD_DIM] = _diff_head_out(o0, o1, lmb, sg_ref[...], lam_init)


def _flash_scratch(rows):
    return [pltpu.VMEM((rows, HEAD_DIM), F32), pltpu.VMEM((rows, LANES), F32),
            pltpu.VMEM((rows, LANES), F32), pltpu.VMEM((rows, HEAD_DIM), F32)]


def _dec_diff_call(page_table, p_s, cache_k, cache_v, knew16, vnew16, lam, sg, cfg, layer, t_new, lam_init):
    nb, n_pages = page_table.shape
    pps = math.gcd(n_pages, ATTN_PAGES)
    kvw = cfg.n_kv_diff * HEAD_DIM
    rows = cfg.n_kv_diff * 2 * cfg.rep_diff * SUBLANES
    q_blk = cfg.off["q_d"] // cfg.d_diff
    kern = functools.partial(_dec_diff_kernel, pps=pps, n_kv=cfg.n_kv_diff, rep=cfg.rep_diff,
                             t_new=t_new, lam_init=lam_init)
    grid_spec = pltpu.PrefetchScalarGridSpec(
        num_scalar_prefetch=1,
        grid=(nb, n_pages // pps),
        in_specs=[pl.BlockSpec((SUBLANES, cfg.d_diff), lambda b, s, pt: (b, q_blk))]
        + 2 * _page_specs(layer, pps, PAGE_SIZE * cfg.n_kv_diff, HEAD_DIM)
        + [pl.BlockSpec((None, PAGE_SIZE, kvw), lambda b, s, pt: (b, 0, 0)),
           pl.BlockSpec((None, PAGE_SIZE, kvw), lambda b, s, pt: (b, 0, 0)),
           pl.BlockSpec(lam.shape, lambda b, s, pt: (0, 0)),
           pl.BlockSpec((1, HEAD_DIM), lambda b, s, pt: (0, 0))],
        out_specs=pl.BlockSpec((SUBLANES, cfg.d_diff), lambda b, s, pt: (b, 0)),
        scratch_shapes=_flash_scratch(rows))
    return pl.pallas_call(
        kern,
        grid_spec=grid_spec,
        out_shape=jax.ShapeDtypeStruct((nb * SUBLANES, cfg.d_diff), F32),
        compiler_params=_params(("arbitrary", "arbitrary"), 48),
        name="diff_attn_sample",
    )(page_table, p_s, *([cache_k] * pps), *([cache_v] * pps), knew16, vnew16, lam, sg.reshape(1, HEAD_DIM))


def _dec_index_kernel(pt_ref, q_ref, w_ref, *refs, pps, t_new, past_len):
    kp = refs[:pps]
    knew_ref, sc_ref = refs[pps:]
    step = pl.program_id(1)
    span = pps * PAGE_SIZE

    def idx_score(s):
        s = jnp.maximum(s, 0.0) * (w_ref[...] * (N_IDX_HEADS ** -0.5))
        return jnp.sum(s.reshape(N_IDX_HEADS, SUBLANES, s.shape[-1]), axis=0)

    k_t = jnp.concatenate([r[...] for r in kp], axis=1).astype(BF16)
    off = pl.multiple_of(step * span, span)
    sc_ref[:, pl.ds(off, span)] = idx_score(jnp.dot(q_ref[...], k_t, preferred_element_type=F32))

    @pl.when(step == pl.num_programs(1) - 1)
    def _():
        tok = lax.broadcasted_iota(jnp.int32, (SUBLANES, LANES), 0)
        col = lax.broadcasted_iota(jnp.int32, (SUBLANES, LANES), 1)
        valid = (col <= tok) & (col < t_new)
        sc_ref[:, past_len:past_len + LANES] = jnp.where(valid, idx_score(_dot_nt(q_ref[...], knew_ref[...])), NEG_INF)


def _select_kernel(s_ref, o_ref, *, k_top):
    o_ref[...] = _selection_bias(s_ref[...], k_top)


def _select_call(scores, k_top):
    r, w = scores.shape
    return pl.pallas_call(
        functools.partial(_select_kernel, k_top=k_top),
        grid=(1,),
        in_specs=[pl.BlockSpec((r, w), lambda i: (0, 0))],
        out_specs=pl.BlockSpec((r, w), lambda i: (0, 0)),
        out_shape=jax.ShapeDtypeStruct((r, w), F32),
        compiler_params=_params(("arbitrary",), 56),
        name="dsa_select_sample",
    )(scores)


def _dec_index_call(page_table, qi_rows, wi_col, cache_kidx_t, kinew16, layer, t_new):
    nb, n_pages = page_table.shape
    pps = math.gcd(n_pages, INDEX_PAGES)
    past_len = n_pages * PAGE_SIZE
    k_top = min(TOPK_MAX, (past_len + t_new) // 4)
    w = past_len + LANES
    rows = N_IDX_HEADS * SUBLANES
    kern = functools.partial(_dec_index_kernel, pps=pps, t_new=t_new, past_len=past_len)
    grid_spec = pltpu.PrefetchScalarGridSpec(
        num_scalar_prefetch=1,
        grid=(nb, n_pages // pps),
        in_specs=[pl.BlockSpec((None, rows, D_IDX), lambda b, s, pt: (b, 0, 0)),
                  pl.BlockSpec((None, rows, 1), lambda b, s, pt: (b, 0, 0))]
        + _page_specs(layer, pps, D_IDX, PAGE_SIZE)
        + [pl.BlockSpec((None, PAGE_SIZE, D_IDX), lambda b, s, pt: (b, 0, 0))],
        out_specs=pl.BlockSpec((None, SUBLANES, w), lambda b, s, pt: (b, 0, 0)))
    scores = pl.pallas_call(
        kern,
        grid_spec=grid_spec,
        out_shape=jax.ShapeDtypeStruct((nb, SUBLANES, w), F32),
        compiler_params=_params(("arbitrary", "arbitrary"), 40),
        name="dsa_index_sample",
    )(page_table, qi_rows, wi_col, *([cache_kidx_t] * pps), kinew16)
    return _select_call(scores.reshape(nb * SUBLANES, w), k_top).reshape(nb, SUBLANES, w)


def _dec_dsa_kernel(pt_ref, q_ref, bias_ref, bnew_ref, *refs, pps, n_kv, rep):
    kp, vp = refs[:pps], refs[pps:2 * pps]
    knew_ref, vnew_ref, o_ref, qs_sc, m_sc, l_sc, acc_sc = refs[2 * pps:]
    step = pl.program_id(1)
    rg = rep * SUBLANES
    groups = [slice(g * rg, (g + 1) * rg) for g in range(n_kv)]
    scale = HEAD_DIM ** -0.5

    @pl.when(step == 0)
    def _():
        for h in range(n_kv * rep):
            qs_sc[h * SUBLANES:(h + 1) * SUBLANES] = q_ref[:, h * HEAD_DIM:(h + 1) * HEAD_DIM]
        _init_flash(m_sc, l_sc, acc_sc)

    def scores(g, k16, bias):
        s = _dot_nt(qs_sc[groups[g]].astype(BF16), k16) * scale
        n = s.shape[-1]
        return (s.reshape(rep, SUBLANES, n) + bias[None]).reshape(rg, n)

    s_all = [scores(g, _cat_pages(kp, g, n_kv), bias_ref[...]) for g in range(n_kv)]
    for g in range(n_kv):
        _online_update(m_sc, l_sc, acc_sc, groups[g], s_all[g], _cat_pages(vp, g, n_kv), True)

    @pl.when(step == pl.num_programs(1) - 1)
    def _():
        for g in range(n_kv):
            c0, c1 = g * HEAD_DIM, (g + 1) * HEAD_DIM
            _online_update(m_sc, l_sc, acc_sc, groups[g], scores(g, knew_ref[:, c0:c1], bnew_ref[...]),
                           vnew_ref[:, c0:c1], True)
        o_all = acc_sc[...] / l_sc[...]
        for h in range(n_kv * rep):
            o_ref[:, h * HEAD_DIM:(h + 1) * HEAD_DIM] = o_all[h * SUBLANES:(h + 1) * SUBLANES]


def _dec_dsa_call(page_table, p_s, bias, cache_k, cache_v, knew16, vnew16, cfg, layer):
    nb, n_pages = page_table.shape
    pps = math.gcd(n_pages, ATTN_PAGES)
    kvw = cfg.n_kv_dsa * HEAD_DIM
    rows = cfg.n_kv_dsa * cfg.rep_dsa * SUBLANES
    span = pps * PAGE_SIZE
    new_blk = n_pages * PAGE_SIZE // LANES
    q_blk = cfg.off["q_s"] // cfg.d_dsa
    kern = functools.partial(_dec_dsa_kernel, pps=pps, n_kv=cfg.n_kv_dsa, rep=cfg.rep_dsa)
    grid_spec = pltpu.PrefetchScalarGridSpec(
        num_scalar_prefetch=1,
        grid=(nb, n_pages // pps),
        in_specs=[pl.BlockSpec((SUBLANES, cfg.d_dsa), lambda b, s, pt: (b, q_blk)),
                  pl.BlockSpec((None, SUBLANES, span), lambda b, s, pt: (b, 0, s)),
                  pl.BlockSpec((None, SUBLANES, LANES), lambda b, s, pt: (b, 0, new_blk))]
        + 2 * _page_specs(layer, pps, PAGE_SIZE * cfg.n_kv_dsa, HEAD_DIM)
        + [pl.BlockSpec((None, PAGE_SIZE, kvw), lambda b, s, pt: (b, 0, 0)),
           pl.BlockSpec((None, PAGE_SIZE, kvw), lambda b, s, pt: (b, 0, 0))],
        out_specs=pl.BlockSpec((SUBLANES, cfg.d_dsa), lambda b, s, pt: (b, 0)),
        scratch_shapes=_flash_scratch(rows))
    return pl.pallas_call(
        kern,
        grid_spec=grid_spec,
        out_shape=jax.ShapeDtypeStruct((nb * SUBLANES, cfg.d_dsa), F32),
        compiler_params=_params(("arbitrary", "arbitrary"), 40),
        name="dsa_attn_sample",
    )(page_table, p_s, bias, bias, *([cache_k] * pps), *([cache_v] * pps), knew16, vnew16)


def _pad_rows(a, rows):
    return jnp.pad(a, ((0, 0), (0, rows - a.shape[1]), (0, 0)))


def kernel(x_prompt, x_sample, cache_diff_k, cache_diff_v, cache_dsa_k, cache_dsa_v, cache_dsa_kidx,
           state_conv, page_table, c_prompt, c_sample, norm_g, w_ada, b_ada, w_in, conv_w,
           diff_lam, diff_subln_g, w_out, final_g):
    nbp, t, d = x_prompt.shape
    nbs, t_new, _ = x_sample.shape
    depth, n_pool = cache_diff_k.shape[:2]
    cfg = _make_cfg(d, cache_diff_k.shape[3], cache_dsa_k.shape[3])
    dc = cfg.d_conv
    off = cfg.off
    kdw, ksw = cfg.n_kv_diff * HEAD_DIM, cfg.n_kv_dsa * HEAD_DIM
    kiwi_w = D_IDX + N_IDX_HEADS
    assert cache_diff_k.shape[2] == PAGE_SIZE and cache_diff_k.shape[4] == HEAD_DIM
    assert state_conv.shape[2] == CONV_WIDTH - 1 and CONV_WIDTH - 1 <= t_new <= SUBLANES
    assert w_in.shape[2] == cfg.n_main + kiwi_w + d and cfg.d_diff == cfg.d_dsa

    w_t = jnp.swapaxes(w_in, 1, 2)
    w_main16 = w_t.astype(BF16)
    w_kiwi16 = jnp.pad(w_t[:, cfg.n_main:cfg.n_main + kiwi_w], ((0, 0), (0, LANES - kiwi_w), (0, 0))).astype(BF16)
    w_z16 = w_t[:, cfg.n_main + kiwi_w:].astype(BF16)
    wo16 = w_out.astype(BF16)
    z_perm = tuple(range(d // PROJ_TN))

    n_c = nbp + nbs
    c_all = _pad_rows(jnp.concatenate([c_prompt, c_sample], axis=0)[None], -(-n_c // SUBLANES) * SUBLANES)[0]
    mod = _mod_call(c_all, w_ada, b_ada).reshape(depth, c_all.shape[0], 3, d)

    ck_d = cache_diff_k.reshape(depth, n_pool, PAGE_SIZE * cfg.n_kv_diff, HEAD_DIM)
    cv_d = cache_diff_v.reshape(depth, n_pool, PAGE_SIZE * cfg.n_kv_diff, HEAD_DIM)
    ck_s = cache_dsa_k.reshape(depth, n_pool, PAGE_SIZE * cfg.n_kv_dsa, HEAD_DIM)
    cv_s = cache_dsa_v.reshape(depth, n_pool, PAGE_SIZE * cfg.n_kv_dsa, HEAD_DIM)
    ck_i = jnp.swapaxes(cache_dsa_kidx, 2, 3)

    xp = x_prompt.reshape(nbp * t, d)
    xs = _pad_rows(x_sample, SUBLANES).reshape(nbs * SUBLANES, d)
    ones_halo = jnp.ones((SUBLANES, dc), F32)
    tm_p = 512
    seq_tiles = t // tm_p
    halo_blocks = tm_p // SUBLANES
    ms = nbs * SUBLANES

    def prompt_halo(name):
        col_blk = off[name] // dc
        return lambda i, j: (jnp.maximum(i * halo_blocks - 1, 0), col_blk)

    def project(h16, l, tm, z_dtype):
        p = _matmul_call(h16, w_main16, l, cfg.perm, F32, tm, PROJ_TN)
        z = _matmul_call(h16, w_z16, l, z_perm, z_dtype, tm, PROJ_TN)
        kiwi = _matmul_call(h16, w_kiwi16, l, (0,), F32, tm, LANES)
        return p, z, kiwi

    p_rows, s_rows = [], []
    for l in range(depth):
        lam_init = LAMBDA_INIT_BASE - LAMBDA_INIT_AMP * math.exp(-LAMBDA_INIT_RATE * l)
        mod_p, mod_s = mod[l, :nbp], mod[l, nbp:n_c]

        h_p = _norm_mod_call(xp, norm_g[l], mod_p, t, 256, BF16)
        p_p, z_p, kiwi_p = project(h_p, l, 1024, BF16)
        yd_p = _diff_prompt_call(p_p, diff_lam[l], diff_subln_g[l], cfg, nbp, t, lam_init)
        ys_p = _dsa_prompt_call(p_p, kiwi_p, cfg, nbp, t)
        xp, ut_p = _mix_out_call(xp, p_p, p_p, p_p, prompt_halo("hx"), prompt_halo("gc"), conv_w[l], yd_p, ys_p,
                                 z_p, mod_p, wo16, l, cfg, t, tm_p, seq_tiles, False)
        p_rows.append((p_p, kiwi_p,
                       ut_p.reshape(nbp, seq_tiles, SUBLANES, dc)[:, -1, SUBLANES - (CONV_WIDTH - 1):]))

        h_s = _norm_mod_call(xs, norm_g[l], mod_s, SUBLANES, SUBLANES, F32).astype(BF16)
        p_s, z_s, kiwi_s = project(h_s, l, ms, F32)
        p_s3 = p_s.reshape(nbs, SUBLANES, cfg.n_main)
        kiwi_s3 = kiwi_s.reshape(nbs, SUBLANES, LANES)

        def new_rows(name, width):
            return _pad_rows(p_s3[:, :, off[name]:off[name] + width], PAGE_SIZE).astype(BF16)

        yd_s = _dec_diff_call(page_table, p_s, ck_d, cv_d, new_rows("k_d", kdw), new_rows("v_d", kdw),
                              diff_lam[l], diff_subln_g[l], cfg, l, t_new, lam_init)
        qi_rows = p_s3[:, :, off["qi"]:off["qi"] + N_IDX_HEADS * D_IDX].reshape(nbs, SUBLANES, N_IDX_HEADS, D_IDX)
        qi_rows = qi_rows.transpose(0, 2, 1, 3).reshape(nbs, N_IDX_HEADS * SUBLANES, D_IDX).astype(BF16)
        wi_col = kiwi_s3[:, :, D_IDX:kiwi_w].transpose(0, 2, 1).reshape(nbs, N_IDX_HEADS * SUBLANES, 1)
        ki_new = _pad_rows(kiwi_s3[:, :, :D_IDX], PAGE_SIZE).astype(BF16)
        bias = _dec_index_call(page_table, qi_rows, wi_col, ck_i, ki_new, l, t_new)
        ys_s = _dec_dsa_call(page_table, p_s, bias, ck_s, cv_s, new_rows("k_s", ksw), new_rows("v_s", ksw),
                             cfg, l)
        state_pad = jnp.pad(state_conv[l], ((0, 0), (SUBLANES - (CONV_WIDTH - 1), 0), (0, 0)))
        xs, ut_s = _mix_out_call(xs, p_s, state_pad.reshape(ms, dc), ones_halo,
                                 lambda i, j: (i, 0), lambda i, j: (0, 0), conv_w[l], yd_s, ys_s,
                                 z_s, mod_s, wo16, l, cfg, SUBLANES, SUBLANES, 0, True)
        s_rows.append((p_s3[:, :t_new], kiwi_s3[:, :t_new], ut_s[0, :, t_new - (CONV_WIDTH - 1):t_new]))

    y_prompt = _rmsnorm_call(xp, final_g, 256).reshape(nbp, t, d)
    y_sample = _rmsnorm_call(xs, final_g, SUBLANES).reshape(nbs, SUBLANES, d)[:, :t_new]

    def stack_p(name, n_kv):
        return jnp.stack([r[0][:, off[name]:off[name] + n_kv * HEAD_DIM].reshape(nbp, t, n_kv, HEAD_DIM)
                          for r in p_rows])

    def stack_s(name, n_kv):
        return jnp.stack([r[0][:, :, off[name]:off[name] + n_kv * HEAD_DIM].reshape(nbs, t_new, n_kv, HEAD_DIM)
                          for r in s_rows])

    return (y_prompt, y_sample,
            stack_p("k_d", cfg.n_kv_diff), stack_p("v_d", cfg.n_kv_diff),
            stack_p("k_s", cfg.n_kv_dsa), stack_p("v_s", cfg.n_kv_dsa),
            jnp.stack([r[1][:, :D_IDX].reshape(nbp, t, D_IDX) for r in p_rows]),
            jnp.stack([r[2] for r in p_rows]),
            stack_s("k_d", cfg.n_kv_diff), stack_s("v_d", cfg.n_kv_diff),
            stack_s("k_s", cfg.n_kv_dsa), stack_s("v_s", cfg.n_kv_dsa),
            jnp.stack([r[1][:, :, :D_IDX] for r in s_rows]),
            jnp.stack([r[2] for r in s_rows]))
```
